```python
import math
import jax, jax.numpy as jnp
from jax import lax
import numpy as np

D_MODEL = 1024
BATCH = 2
SEQ = 8192
DEPTH = 1
DEC_BATCH = 128
DEC_SEQ = 8
PAST_LEN = 8192
PAGE_SIZE = 128

N_META = 16
SB_HEADS = 8
SB_HEAD_DIM = 64
SB_WIDTH = SB_HEADS * SB_HEAD_DIM
SB_BLOCK = 128
SB_SCALE = 1.0 / math.sqrt(SB_HEAD_DIM)
SB_BIAS_INIT = -7.0
RW_HEADS = 8
RW_HEAD_DIM = 64
RW_WIDTH = RW_HEADS * RW_HEAD_DIM
DECAY_LORA = 64
AAA_LORA = 64
GATE_LORA = 128
RW_PROJ = 3 * RW_WIDTH + DECAY_LORA + AAA_LORA + GATE_LORA
GN_EPS = 64e-5
N_IN = 3 * SB_WIDTH + RW_PROJ
N_EXPERTS = 32
TOP_K = 4
D_EXPERT = D_MODEL
SWIGLU_LIMIT = 7.0
SWIGLU_ALPHA = 1.702
MOE_BLOCK = 128
RMS_EPS = 1e-5

kernel_name = "stickbreak_rwkv7_gated_moe_step"


def rms_norm(x, g):
    xf = x.astype(jnp.float32)
    y = xf * lax.rsqrt(jnp.mean(xf * xf, axis=-1, keepdims=True) + RMS_EPS)
    return (y * g.astype(jnp.float32)).astype(x.dtype)


def stick_breaking(q, k, v, q_pos, k_pos, bias):
    z = jnp.einsum('...qhd,...khd->...hqk', q.astype(jnp.float32), k.astype(jnp.float32)) * SB_SCALE
    z = z + bias.astype(jnp.float32)[:, None, None]
    visible = k_pos[None, :] < q_pos[:, None]
    log_beta = jax.nn.log_sigmoid(z)
    log_keep = jnp.where(visible, jax.nn.log_sigmoid(-z), 0.0)
    log_stick = lax.cumsum(log_keep, axis=z.ndim - 1, reverse=True) - log_keep
    weights = jnp.where(visible, jnp.exp(log_beta + log_stick), 0.0)
    out = jnp.einsum('...hqk,...khd->...qhd', weights, v.astype(jnp.float32))
    return out.astype(q.dtype)


def prompt_attention(q, k, v, bias):
    b, t, h, d = q.shape
    n_blocks = (t - N_META) // SB_BLOCK
    pos = jnp.arange(t)
    meta_out = stick_breaking(q[:, :N_META], k[:, :N_META], v[:, :N_META], pos[:N_META], pos[:N_META], bias)
    q_blocks = jnp.moveaxis(q[:, N_META:].reshape(b, n_blocks, SB_BLOCK, h, d), 1, 0)
    pos_blocks = pos[N_META:].reshape(n_blocks, SB_BLOCK)
    blocks_out = lax.map(lambda a: stick_breaking(a[0], k, v, a[1], pos, bias), (q_blocks, pos_blocks))
    blocks_out = jnp.moveaxis(blocks_out, 0, 1).reshape(b, t - N_META, h, d)
    return jnp.concatenate([meta_out, blocks_out], axis=1)


def paged_attention(q, k_new, v_new, k_pool, v_pool, page_table, bias):
    past_len = page_table.shape[1] * k_pool.shape[1]
    ds = q.shape[1]
    q_pos = past_len + jnp.arange(ds)
    k_pos = jnp.arange(past_len + ds)

    def one_sequence(args):
        q_s, k_s, v_s, pages = args
        k_past = k_pool[pages].reshape((past_len,) + k_pool.shape[2:]).astype(k_s.dtype)
        v_past = v_pool[pages].reshape((past_len,) + v_pool.shape[2:]).astype(v_s.dtype)
        return stick_breaking(q_s, jnp.concatenate([k_past, k_s], axis=0),
                              jnp.concatenate([v_past, v_s], axis=0), q_pos, k_pos, bias)

    return lax.map(one_sequence, (q, k_new, v_new, page_table))


def rwkv7_mix(p, shift0, wkv0, mu, w0, decay_up, a0, aaa_up, gate_up, k_k, k_a, r_k, ln_w, ln_b):
    out_dtype = p.dtype
    f32 = jnp.float32
    p = p.astype(f32)
    prev = jnp.concatenate([shift0.astype(f32)[:, None], p[:, :-1]], axis=1)
    xm = p + (prev - p) * mu
    c1, c2, c3 = RW_WIDTH, 2 * RW_WIDTH, 3 * RW_WIDTH
    c4 = c3 + DECAY_LORA
    c5 = c4 + AAA_LORA
    r, k, v = xm[..., :c1], xm[..., c1:c2], xm[..., c2:c3]
    w_lo, a_lo, g_lo = xm[..., c3:c4], xm[..., c4:c5], xm[..., c5:]
    log_w = -jax.nn.softplus(-(w0 + jnp.tanh(w_lo) @ decay_up)) - 0.5
    decay = jnp.exp(-jnp.exp(log_w))
    a = jax.nn.sigmoid(a0 + a_lo @ aaa_up)
    g = jax.nn.sigmoid(g_lo) @ gate_up

    def heads(t):
        return t.reshape(t.shape[:-1] + (RW_HEADS, RW_HEAD_DIM))

    kk = heads(k * k_k)
    kk = kk / jnp.maximum(jnp.sqrt(jnp.sum(kk * kk, axis=-1, keepdims=True)), 1e-12)
    k = heads(k * (1.0 + (a - 1.0) * k_a))
    r, v, decay, a = heads(r), heads(v), heads(decay), heads(a)

    def step(S, inp):
        r_t, w_t, k_t, v_t, kk_t, a_t = inp
        sa = jnp.einsum('bhvk,bhk->bhv', S, -kk_t)
        S = (S * w_t[:, :, None, :] + sa[..., None] * (kk_t * a_t)[:, :, None, :]
             + v_t[..., None] * k_t[:, :, None, :])
        return S, jnp.einsum('bhvk,bhk->bhv', S, r_t)

    tm = lambda t: jnp.moveaxis(t, 1, 0)
    wkv, y = lax.scan(step, wkv0.astype(f32), (tm(r), tm(decay), tm(k), tm(v), tm(kk), tm(a)))
    y = jnp.moveaxis(y, 0, 1)
    mean = jnp.mean(y, axis=-1, keepdims=True)
    var = jnp.mean(jnp.square(y - mean), axis=-1, keepdims=True)
    y = ((y - mean) * lax.rsqrt(var + GN_EPS)).reshape(y.shape[:2] + (RW_WIDTH,)) * ln_w + ln_b
    bonus = (jnp.sum(r * k * r_k, axis=-1, keepdims=True) * v).reshape(y.shape)
    y = (y + bonus) * g
    return y.astype(out_dtype), wkv.astype(wkv0.dtype), p[:, -1].astype(out_dtype)


def mixer_block(x, attend, shift0, wkv0, norm_g, w_in, rw, w_gate, b_gate, w_branch, w_out):
    h = rms_norm(x, norm_g)
    proj = h @ w_in
    heads = lambda t: t.reshape(t.shape[:-1] + (SB_HEADS, SB_HEAD_DIM))
    q = heads(proj[..., :SB_WIDTH])
    k = heads(proj[..., SB_WIDTH:2 * SB_WIDTH])
    v = heads(proj[..., 2 * SB_WIDTH:3 * SB_WIDTH])
    attn = attend(q, k, v).reshape(h.shape[:-1] + (SB_WIDTH,))
    rw_out, wkv, shift = rwkv7_mix(proj[..., 3 * SB_WIDTH:], shift0, wkv0, *rw)
    gates = jax.nn.sigmoid(h @ w_gate + b_gate)
    merged = gates[..., :D_MODEL] * (attn @ w_branch[0]) + gates[..., D_MODEL:] * (rw_out @ w_branch[1])
    return x + merged @ w_out, k, v, wkv, shift


def routed_experts(h, w_router, b_router, w1, b1, w2, b2):
    n = h.shape[0]
    logits = h.astype(jnp.float32) @ w_router.astype(jnp.float32) + b_router.astype(jnp.float32)
    top_val, top_idx = lax.top_k(logits, TOP_K)
    gate = jax.nn.softmax(top_val, axis=-1).astype(h.dtype)
    nk = n * TOP_K
    flat_e = top_idx.reshape(-1)
    flat_tok = jnp.arange(nk, dtype=jnp.int32) // TOP_K
    flat_g = gate.reshape(-1)
    order = jnp.argsort(flat_e)
    e_sorted = flat_e[order]
    counts = jnp.bincount(flat_e, length=N_EXPERTS)
    padded = (counts + MOE_BLOCK - 1) // MOE_BLOCK * MOE_BLOCK
    pad_end = jnp.cumsum(padded)
    pad_start = pad_end - padded
    start = jnp.cumsum(counts) - counts
    dest = pad_start[e_sorted] + jnp.arange(nk) - start[e_sorted]
    n_blocks = -(-nk // MOE_BLOCK) + N_EXPERTS
    row_tok = jnp.zeros((n_blocks * MOE_BLOCK,), jnp.int32).at[dest].set(flat_tok[order])
    row_g = jnp.zeros((n_blocks * MOE_BLOCK,), h.dtype).at[dest].set(flat_g[order])
    block_e = jnp.minimum(jnp.searchsorted(pad_end, jnp.arange(n_blocks) * MOE_BLOCK, side='right'),
                          N_EXPERTS - 1)

    def expert_block(args):
        e, tok, g = args
        gu = h[tok] @ w1[e] + b1[e]
        gt = jnp.minimum(gu[:, :D_EXPERT], SWIGLU_LIMIT)
        up = jnp.clip(gu[:, D_EXPERT:], -SWIGLU_LIMIT, SWIGLU_LIMIT)
        act = (up + 1.0) * (gt * jax.nn.sigmoid(SWIGLU_ALPHA * gt))
        return (act @ w2[e] + b2[e]) * g[:, None]

    outs = lax.map(expert_block, (block_e, row_tok.reshape(n_blocks, MOE_BLOCK),
                                  row_g.reshape(n_blocks, MOE_BLOCK)))
    return jnp.zeros_like(h).at[row_tok].add(outs.reshape(-1, D_MODEL))


def moe_block(x, norm_g, w_router, b_router, w1, b1, w2, b2):
    h = rms_norm(x, norm_g)
    y = routed_experts(h.reshape(-1, D_MODEL), w_router, b_router, w1, b1, w2, b2)
    return x + y.reshape(x.shape)


def setup_inputs(seed: int = 0) -> dict:
    key = jax.random.key(seed)
    keys = jax.random.split(key, 40)
    f32 = jnp.float32

    def nrm(i, shape, scale):
        return scale * jax.random.normal(keys[i], shape, f32)

    n_pages = PAST_LEN // PAGE_SIZE
    n_phys = (DEC_BATCH * n_pages * 5) // 4
    page_table = jax.random.permutation(keys[7], n_phys)[:DEC_BATCH * n_pages]
    page_table = page_table.reshape(DEC_BATCH, n_pages).astype(jnp.int32)
    return {
        'x_prompt': nrm(0, (BATCH, SEQ, D_MODEL), 1.0),
        'x_sample': nrm(1, (DEC_BATCH, DEC_SEQ, D_MODEL), 1.0),
        'cache_k': nrm(2, (DEPTH, n_phys, PAGE_SIZE, SB_HEADS, SB_HEAD_DIM), 1.0),
        'cache_v': nrm(3, (DEPTH, n_phys, PAGE_SIZE, SB_HEADS, SB_HEAD_DIM), 1.0),
        'state_wkv': nrm(4, (DEPTH, DEC_BATCH, RW_HEADS, RW_HEAD_DIM, RW_HEAD_DIM), 0.3),
        'state_shift': nrm(5, (DEPTH, DEC_BATCH, RW_PROJ), 1.0),
        'page_table': page_table,
        'meta': nrm(6, (N_META, D_MODEL), 1.0),
        'norm_mix': 1.0 + nrm(8, (DEPTH, D_MODEL), 0.02),
        'w_in': nrm(9, (DEPTH, D_MODEL, N_IN), D_MODEL ** -0.5),
        'sb_bias': SB_BIAS_INIT + nrm(33, (DEPTH, SB_HEADS), 0.5),
        'rw_mu': jax.random.uniform(keys[10], (DEPTH, RW_PROJ), f32),
        'rw_w0': 0.5 + nrm(11, (DEPTH, RW_WIDTH), 0.5),
        'rw_decay_up': nrm(12, (DEPTH, DECAY_LORA, RW_WIDTH), 0.5 * DECAY_LORA ** -0.5),
        'rw_a0': nrm(13, (DEPTH, RW_WIDTH), 0.5),
        'rw_aaa_up': nrm(14, (DEPTH, AAA_LORA, RW_WIDTH), 0.5 * AAA_LORA ** -0.5),
        'rw_gate_up': nrm(15, (DEPTH, GATE_LORA, RW_WIDTH), GATE_LORA ** -0.5),
        'rw_k_k': 0.85 + nrm(16, (DEPTH, RW_WIDTH), 0.05),
        'rw_k_a': 1.0 + nrm(17, (DEPTH, RW_WIDTH), 0.05),
        'rw_r_k': nrm(18, (DEPTH, RW_HEADS, RW_HEAD_DIM), 0.1),
        'rw_ln_w': 1.0 + nrm(19, (DEPTH, RW_WIDTH), 0.02),
        'rw_ln_b': nrm(20, (DEPTH, RW_WIDTH), 0.02),
        'w_gate': nrm(21, (DEPTH, D_MODEL, 2 * D_MODEL), D_MODEL ** -0.5),
        'b_gate': nrm(22, (DEPTH, 2 * D_MODEL), 0.02),
        'w_branch': nrm(23, (DEPTH, 2, SB_WIDTH, D_MODEL), SB_WIDTH ** -0.5),
        'w_out': nrm(24, (DEPTH, D_MODEL, D_MODEL), D_MODEL ** -0.5),
        'norm_ffn': 1.0 + nrm(25, (DEPTH, D_MODEL), 0.02),
        'w_router': nrm(26, (DEPTH, D_MODEL, N_EXPERTS), D_MODEL ** -0.5),
        'b_router': nrm(27, (DEPTH, N_EXPERTS), 0.01),
        'w1': nrm(28, (DEPTH, N_EXPERTS, D_MODEL, 2 * D_EXPERT), D_MODEL ** -0.5),
        'b1': nrm(29, (DEPTH, N_EXPERTS, 2 * D_EXPERT), 0.01),
        'w2': nrm(30, (DEPTH, N_EXPERTS, D_EXPERT, D_MODEL), D_EXPERT ** -0.5),
        'b2': nrm(31, (DEPTH, N_EXPERTS, D_MODEL), 0.01),
        'norm_final': 1.0 + nrm(32, (D_MODEL,), 0.02),
    }


def reference(x_prompt, x_sample, cache_k, cache_v, state_wkv, state_shift, page_table, meta,
              norm_mix, w_in, sb_bias, rw_mu, rw_w0, rw_decay_up, rw_a0, rw_aaa_up, rw_gate_up, rw_k_k,
              rw_k_a, rw_r_k, rw_ln_w, rw_ln_b, w_gate, b_gate, w_branch, w_out, norm_ffn, w_router,
              b_router, w1, b1, w2, b2, norm_final):
    n_prompt = x_prompt.shape[0]
    meta_rows = jnp.broadcast_to(meta.astype(x_prompt.dtype)[None], (n_prompt, N_META, D_MODEL))
    xp = jnp.concatenate([meta_rows, x_prompt], axis=1)
    xs = x_sample
    shift_zero = jnp.zeros((n_prompt, RW_PROJ), x_prompt.dtype)
    wkv_zero = jnp.zeros((n_prompt, RW_HEADS, RW_HEAD_DIM, RW_HEAD_DIM), state_wkv.dtype)
    kp_l, vp_l, wp_l, sp_l, ks_l, vs_l, ws_l, ss_l = [], [], [], [], [], [], [], []
    for l in range(DEPTH):
        rw = (rw_mu[l], rw_w0[l], rw_decay_up[l], rw_a0[l], rw_aaa_up[l], rw_gate_up[l],
              rw_k_k[l], rw_k_a[l], rw_r_k[l], rw_ln_w[l], rw_ln_b[l])
        attend_p = lambda q, k, v, l=l: prompt_attention(q, k, v, sb_bias[l])
        xp, k_p, v_p, wkv_p, sh_p = mixer_block(xp, attend_p, shift_zero, wkv_zero, norm_mix[l],
                                                w_in[l], rw, w_gate[l], b_gate[l], w_branch[l], w_out[l])
        xp = moe_block(xp, norm_ffn[l], w_router[l], b_router[l], w1[l], b1[l], w2[l], b2[l])
        attend_s = lambda q, k, v, l=l: paged_attention(q, k, v, cache_k[l], cache_v[l], page_table, sb_bias[l])
        xs, k_s, v_s, wkv_s, sh_s = mixer_block(xs, attend_s, state_shift[l], state_wkv[l], norm_mix[l],
                                                w_in[l], rw, w_gate[l], b_gate[l], w_branch[l], w_out[l])
        xs = moe_block(xs, norm_ffn[l], w_router[l], b_router[l], w1[l], b1[l], w2[l], b2[l])
        kp_l.append(k_p); vp_l.append(v_p); wp_l.append(wkv_p); sp_l.append(sh_p)
        ks_l.append(k_s); vs_l.append(v_s); ws_l.append(wkv_s); ss_l.append(sh_s)
    y_prompt = rms_norm(xp[:, N_META:], norm_final)
    y_sample = rms_norm(xs, norm_final)
    return (y_prompt, y_sample, jnp.stack(kp_l), jnp.stack(vp_l), jnp.stack(wp_l), jnp.stack(sp_l),
            jnp.stack(ks_l), jnp.stack(vs_l), jnp.stack(ws_l), jnp.stack(ss_l))
```

```python
import functools
import math

import numpy as np
import jax
import jax.numpy as jnp
from jax import lax
from jax.experimental import pallas as pl
from jax.experimental.pallas import tpu as pltpu

F32 = jnp.float32
BF16 = jnp.bfloat16
HIGHEST = lax.Precision.HIGHEST

D_MODEL = 1024
N_META = 16
HEADS = 8
HEAD_DIM = 64
WIDTH = HEADS * HEAD_DIM
SB_SCALE = 1.0 / math.sqrt(HEAD_DIM)
DECAY_LORA = 64
AAA_LORA = 64
GATE_LORA = 128
RW_PROJ = 3 * WIDTH + DECAY_LORA + AAA_LORA + GATE_LORA
GN_EPS = 64e-5
N_EXPERTS = 32
TOP_K = 4
SWIGLU_LIMIT = 7.0
SWIGLU_ALPHA = 1.702
RMS_EPS = 1e-5

ROW_TILE = 256
ATT_TILE = 256
RWKV_CHUNK = 64
PAGES_PER_STEP = 8
MOE_TILE = 256
VMEM_LIMIT = 56 * 1024 * 1024

NT_DIMS = (((1,), (1,)), ((), ()))
TN_DIMS = (((0,), (0,)), ((), ()))
NN_DIMS = (((1,), (0,)), ((), ()))


def _dot(a, b):
    return jnp.dot(a, b, preferred_element_type=F32)


def _dotf(a, b, dims=NN_DIMS):
    return lax.dot_general(a, b, dims, precision=HIGHEST, preferred_element_type=F32)


def _split3(x):
    hi = x.astype(BF16)
    r1 = x - hi.astype(F32)
    mid = r1.astype(BF16)
    lo = (r1 - mid.astype(F32)).astype(BF16)
    return hi, mid, lo


def _dot_exact_rhs(x, m):
    hi, mid, lo = _split3(x)
    return _dot(hi, m) + _dot(mid, m) + _dot(lo, m)


def _dot_exact_lhs(m, x):
    hi, mid, lo = _split3(x)
    return _dot(m, hi) + _dot(m, mid) + _dot(m, lo)


def _neg_softplus(z):
    return jnp.minimum(-z, 0.0) - jnp.log1p(jnp.exp(-jnp.abs(z)))


def _rms(x, g):
    return x * lax.rsqrt(jnp.mean(x * x, axis=-1, keepdims=True) + RMS_EPS) * g


def _proj_kernel(x_ref, g_ref, wqkv_ref, wp_ref, wg_ref, bg_ref,
                 q_ref, k_ref, v_ref, kb_ref, vb_ref, p_ref, gate_ref):
    hb = _rms(x_ref[...], g_ref[...]).astype(BF16)
    qkv = _dot(hb, wqkv_ref[...])
    k = qkv[:, WIDTH:2 * WIDTH]
    v = qkv[:, 2 * WIDTH:]
    q_ref[...] = (qkv[:, :WIDTH] * SB_SCALE).astype(BF16)
    k_ref[...] = k
    v_ref[...] = v
    kb_ref[...] = k.astype(BF16)
    vb_ref[...] = v.astype(BF16)
    p_ref[...] = _dot(hb, wp_ref[...])
    gate_ref[...] = jax.nn.sigmoid(_dot(hb, wg_ref[...]) + bg_ref[...])


def _projections(x, norm_g, w_in, w_gate, b_gate):
    rows = x.shape[0]
    tm = ROW_TILE
    wqkv = w_in[:, :3 * WIDTH].astype(BF16)
    wp = w_in[:, 3 * WIDTH:].astype(BF16)
    wg = w_gate.astype(BF16)
    row_spec = lambda n: pl.BlockSpec((tm, n), lambda i: (i, 0))
    full = lambda a: pl.BlockSpec(a.shape, lambda i: (0,) * a.ndim)
    g2 = norm_g.reshape(1, D_MODEL)
    bg = b_gate.reshape(1, 2 * D_MODEL)
    return pl.pallas_call(
        _proj_kernel,
        grid=(rows // tm,),
        in_specs=[row_spec(D_MODEL), full(g2), full(wqkv), full(wp), full(wg), full(bg)],
        out_specs=[row_spec(WIDTH)] * 5 + [row_spec(RW_PROJ), row_spec(2 * D_MODEL)],
        out_shape=[jax.ShapeDtypeStruct((rows, WIDTH), BF16),
                   jax.ShapeDtypeStruct((rows, WIDTH), F32),
                   jax.ShapeDtypeStruct((rows, WIDTH), F32),
                   jax.ShapeDtypeStruct((rows, WIDTH), BF16),
                   jax.ShapeDtypeStruct((rows, WIDTH), BF16),
                   jax.ShapeDtypeStruct((rows, RW_PROJ), F32),
                   jax.ShapeDtypeStruct((rows, 2 * D_MODEL), F32)],
        compiler_params=pltpu.CompilerParams(dimension_semantics=("arbitrary",),
                                             vmem_limit_bytes=VMEM_LIMIT),
        name="projections",
    )(x, g2, wqkv, wp, wg, bg)


def _stick_tile(z, visible, carry, suffix_mat):
    log_keep = _neg_softplus(z)
    log_beta = log_keep + z
    if visible is not None:
        log_keep = jnp.where(visible, log_keep, 0.0)
    hi = log_keep.astype(BF16)
    lo = (log_keep - hi.astype(F32)).astype(BF16)
    log_stick = _dot(hi, suffix_mat) + _dot(lo, suffix_mat) + carry
    w = jnp.exp(log_beta + log_stick)
    if visible is not None:
        w = jnp.where(visible, w, 0.0)
    return w, jnp.sum(log_keep, axis=-1, keepdims=True)


def _suffix_matrix(n):
    j = np.arange(n)
    return jnp.asarray((j[:, None] > j[None, :]).astype(np.float32), dtype=BF16)


def _prompt_attn_kernel(qi_ref, kj_ref, bias_ref, q_ref, k_ref, v_ref, m_ref, o_ref,
                        acc_ref, carry_ref, *, tile, pad):
    t = pl.program_id(1)
    i = qi_ref[t]
    j = kj_ref[t]

    @pl.when(j == i)
    def _():
        acc_ref[...] = jnp.zeros_like(acc_ref)
        carry_ref[...] = jnp.zeros_like(carry_ref)

    t_far = pl.num_programs(1) * tile

    def process(masked):
        visible = None
        if masked:
            q_pos = i * tile + lax.broadcasted_iota(jnp.int32, (tile, tile), 0)
            k_pos = j * tile + lax.broadcasted_iota(jnp.int32, (tile, tile), 1)
            visible = jnp.where(k_pos >= pad, k_pos, t_far) < q_pos
        for h in range(HEADS):
            sl = slice(h * HEAD_DIM, (h + 1) * HEAD_DIM)
            s = lax.dot_general(q_ref[:, sl], k_ref[:, sl], NT_DIMS, preferred_element_type=F32)
            w, total = _stick_tile(s + bias_ref[h], visible, carry_ref[h], m_ref[...])
            acc_ref[:, sl] += _dot(w.astype(BF16), v_ref[:, sl])
            carry_ref[h] += total

    needs_mask = jnp.logical_or(j == i, j == 0)
    pl.when(needs_mask)(lambda: process(True))
    pl.when(jnp.logical_not(needs_mask))(lambda: process(False))

    @pl.when(j == 0)
    def _():
        o_ref[...] = acc_ref[...].astype(o_ref.dtype)


def _prompt_attention(qb, kb, vb, bias, batch, t_pad, pad):
    tile = ATT_TILE
    nq = t_pad // tile
    qi = np.concatenate([np.full(i + 1, i) for i in range(nq)]).astype(np.int32)
    kj = np.concatenate([np.arange(i, -1, -1) for i in range(nq)]).astype(np.int32)
    grid_spec = pltpu.PrefetchScalarGridSpec(
        num_scalar_prefetch=2,
        grid=(batch, len(qi)),
        in_specs=[
            pl.BlockSpec(memory_space=pltpu.SMEM),
            pl.BlockSpec((tile, WIDTH), lambda b, t, qi, kj: (b * nq + qi[t], 0)),
            pl.BlockSpec((tile, WIDTH), lambda b, t, qi, kj: (b * nq + kj[t], 0)),
            pl.BlockSpec((tile, WIDTH), lambda b, t, qi, kj: (b * nq + kj[t], 0)),
            pl.BlockSpec((tile, tile), lambda b, t, qi, kj: (0, 0)),
        ],
        out_specs=pl.BlockSpec((tile, WIDTH), lambda b, t, qi, kj: (b * nq + qi[t], 0)),
        scratch_shapes=[pltpu.VMEM((tile, WIDTH), F32), pltpu.VMEM((HEADS, tile, 1), F32)],
    )
    return pl.pallas_call(
        functools.partial(_prompt_attn_kernel, tile=tile, pad=pad),
        grid_spec=grid_spec,
        out_shape=jax.ShapeDtypeStruct((batch * t_pad, WIDTH), BF16),
        compiler_params=pltpu.CompilerParams(dimension_semantics=("arbitrary", "arbitrary"),
                                             vmem_limit_bytes=VMEM_LIMIT),
        name="prompt_attention",
    )(jnp.asarray(qi), jnp.asarray(kj), bias, qb, kb, vb, _suffix_matrix(tile))


def _paged_attn_kernel(pt_ref, bias_ref, hmask_ref, m_ref, q_ref, kn_ref, vn_ref, *rest,
                       n_new, page, pages_per_step):
    page_refs = rest[:2 * pages_per_step]
    o_ref = rest[2 * pages_per_step]
    qbd_ref, acc_ref, carry_ref, knew_ref, vnew_ref = rest[2 * pages_per_step + 1:]
    s = pl.program_id(1)
    rows = HEADS * n_new

    def process(kb, vb, visible):
        z = lax.dot_general(qbd_ref[...], kb, NT_DIMS, preferred_element_type=F32) + bias_ref[...]
        w, total = _stick_tile(z, visible, carry_ref[...], m_ref[...])
        acc_ref[...] += _dot(w.astype(BF16), vb)
        carry_ref[...] += total

    @pl.when(s == 0)
    def _():
        q_rep = jnp.concatenate([q_ref[...]] * HEADS, axis=0)
        qbd_ref[...] = jnp.where(hmask_ref[...] > 0.0, q_rep, jnp.zeros_like(q_rep))
        acc_ref[...] = jnp.zeros_like(acc_ref)
        carry_ref[...] = jnp.zeros_like(carry_ref)
        knew_ref[...] = jnp.zeros_like(knew_ref)
        vnew_ref[...] = jnp.zeros_like(vnew_ref)
        knew_ref[0:n_new, :] = kn_ref[...]
        vnew_ref[0:n_new, :] = vn_ref[...]
        t_idx = lax.rem(lax.broadcasted_iota(jnp.int32, (rows, page), 0), n_new)
        c_idx = lax.broadcasted_iota(jnp.int32, (rows, page), 1)
        process(knew_ref[...].astype(BF16), vnew_ref[...].astype(BF16), c_idx < t_idx)

    for p in range(pages_per_step):
        process(page_refs[2 * p][0].astype(BF16), page_refs[2 * p + 1][0].astype(BF16), None)

    @pl.when(s == pl.num_programs(1) - 1)
    def _():
        own = acc_ref[...] * hmask_ref[...]
        o_ref[...] = own.reshape(HEADS, n_new, WIDTH).sum(axis=0).astype(o_ref.dtype)


def _paged_attention(qb, k_new, v_new, k_pool, v_pool, page_table, bias, row0, n_seq, n_new):
    n_pages = page_table.shape[1]
    page = k_pool.shape[1]
    pps = PAGES_PER_STEP
    steps = n_pages // pps
    rows = HEADS * n_new
    blk0 = row0 // n_new
    hmask = jnp.asarray((np.arange(rows)[:, None] // n_new == np.arange(WIDTH)[None, :] // HEAD_DIM)
                        .astype(np.float32))
    bias_col = jnp.repeat(bias.astype(F32), n_new).reshape(rows, 1)
    const = lambda a: pl.BlockSpec(a.shape, lambda b, s, pt: (0,) * a.ndim)
    new_spec = pl.BlockSpec((n_new, WIDTH), lambda b, s, pt: (blk0 + b, 0))

    def page_spec(p):
        return pl.BlockSpec((1, page, WIDTH),
                            lambda b, s, pt: (pt[b, n_pages - 1 - (s * pps + p)], 0, 0))

    page_specs, page_args = [], []
    for p in range(pps):
        page_specs += [page_spec(p), page_spec(p)]
        page_args += [k_pool, v_pool]
    m = _suffix_matrix(page)
    grid_spec = pltpu.PrefetchScalarGridSpec(
        num_scalar_prefetch=1,
        grid=(n_seq, steps),
        in_specs=[const(bias_col), const(hmask), const(m), new_spec, new_spec, new_spec] + page_specs,
        out_specs=pl.BlockSpec((n_new, WIDTH), lambda b, s, pt: (b, 0)),
        scratch_shapes=[pltpu.VMEM((rows, WIDTH), BF16), pltpu.VMEM((rows, WIDTH), F32),
                        pltpu.VMEM((rows, 1), F32), pltpu.VMEM((page, WIDTH), F32),
                        pltpu.VMEM((page, WIDTH), F32)],
    )
    return pl.pallas_call(
        functools.partial(_paged_attn_kernel, n_new=n_new, page=page, pages_per_step=pps),
        grid_spec=grid_spec,
        out_shape=jax.ShapeDtypeStruct((n_seq * n_new, WIDTH), BF16),
        compiler_params=pltpu.CompilerParams(dimension_semantics=("arbitrary", "arbitrary"),
                                             vmem_limit_bytes=VMEM_LIMIT),
        name="paged_attention",
    )(page_table, bias_col, hmask, m, qb, k_new, v_new, *page_args)


def _rwkv_kernel(p_ref, shift_ref, s0_ref, mu_ref, w0_ref, dup_ref, a0_ref, aup_ref, gup_ref,
                 kkw_ref, kaw_ref, rk_ref, lnw_ref, lnb_ref, hsum_ref,
                 y_ref, sout_ref, state_ref, prev_ref, *, chunk):
    c = pl.program_id(1)

    @pl.when(c == 0)
    def _():
        for h in range(HEADS):
            state_ref[h] = s0_ref[0, h].T
        prev_ref[...] = shift_ref[0]

    p = p_ref[...]
    row = lax.broadcasted_iota(jnp.int32, p.shape, 0)
    prev = jnp.where(row == 0, prev_ref[...], pltpu.roll(p, 1, 0))
    prev_ref[...] = p[chunk - 1:chunk, :]
    xm = p + (prev - p) * mu_ref[...]

    c1, c2, c3 = WIDTH, 2 * WIDTH, 3 * WIDTH
    c4 = c3 + DECAY_LORA
    c5 = c4 + AAA_LORA
    r, k, v = xm[:, :c1], xm[:, c1:c2], xm[:, c2:c3]
    w_lo, a_lo, g_lo = xm[:, c3:c4], xm[:, c4:c5], xm[:, c5:]
    log_w = _neg_softplus(-(w0_ref[...] + _dotf(jnp.tanh(w_lo), dup_ref[...]))) - 0.5
    log_decay = -jnp.exp(log_w)
    a = jax.nn.sigmoid(a0_ref[...] + _dotf(a_lo, aup_ref[...]))
    g = _dotf(jax.nn.sigmoid(g_lo), gup_ref[...])

    hsum = hsum_ref[...]
    kk = k * kkw_ref[...]
    kk = kk / jnp.maximum(jnp.sqrt(_dot_exact_rhs(kk * kk, hsum)), 1e-12)
    k = k * (1.0 + (a - 1.0) * kaw_ref[...])

    ti = lax.broadcasted_iota(jnp.int32, (chunk, chunk), 0)
    tj = lax.broadcasted_iota(jnp.int32, (chunk, chunk), 1)
    incl = ti >= tj
    strict = ti > tj
    cum = _dot_exact_lhs(jnp.where(incl, 1.0, 0.0).astype(BF16), log_decay)
    p_incl = jnp.exp(cum)
    p_inv = jnp.exp(-cum)
    p_end = p_incl[chunk - 1:chunk, :]
    a_t = -kk * jnp.exp(cum - log_decay)
    r_t = r * p_incl
    b_t = kk * a * p_inv
    k_t = k * p_inv
    b_e = b_t * p_end
    k_e = k_t * p_end
    eye = ti == tj

    ys = []
    for h in range(HEADS):
        sl = slice(h * HEAD_DIM, (h + 1) * HEAD_DIM)
        s0 = state_ref[h]
        ah, rh, bh, kh, vh = a_t[:, sl], r_t[:, sl], b_t[:, sl], k_t[:, sl], v[:, sl]
        n_ab = jnp.where(strict, _dotf(ah, bh, NT_DIMS), 0.0)
        a_ak = jnp.where(strict, _dotf(ah, kh, NT_DIMS), 0.0)
        a_rb = jnp.where(incl, _dotf(rh, bh, NT_DIMS), 0.0)
        a_rk = jnp.where(incl, _dotf(rh, kh, NT_DIMS), 0.0)
        inv = jnp.where(eye, 1.0, 0.0) + n_ab
        power = n_ab
        for _ in range(max(int(math.log2(chunk)) - 1, 0)):
            power = _dotf(power, power)
            inv = inv + _dotf(inv, power)
        u = _dotf(inv, _dotf(ah, s0) + _dotf(a_ak, vh))
        ys.append(_dotf(rh, s0) + _dotf(a_rb, u) + _dotf(a_rk, vh))
        pe = p_end[:, sl]
        pe_col = jnp.sum(jnp.where(lax.broadcasted_iota(jnp.int32, (HEAD_DIM, HEAD_DIM), 0)
                                   == lax.broadcasted_iota(jnp.int32, (HEAD_DIM, HEAD_DIM), 1),
                                   pe, 0.0), axis=1, keepdims=True)
        state_ref[h] = (pe_col * s0 + _dotf(b_e[:, sl], u, TN_DIMS) + _dotf(k_e[:, sl], vh, TN_DIMS))
    y = jnp.concatenate(ys, axis=1)

    inv_n = 1.0 / HEAD_DIM
    mean = _dot_exact_rhs(y, hsum) * inv_n
    yc = y - mean
    var = _dot_exact_rhs(yc * yc, hsum) * inv_n
    yn = yc * lax.rsqrt(var + GN_EPS) * lnw_ref[...] + lnb_ref[...]
    bonus = _dot_exact_rhs(r * k * rk_ref[...], hsum) * v
    y_ref[...] = ((yn + bonus) * g).astype(y_ref.dtype)

    @pl.when(c == pl.num_programs(1) - 1)
    def _():
        for h in range(HEADS):
            sout_ref[0, h] = state_ref[h].T


def _rwkv(p_all, row0, n_seq, t_len, chunk, shift0, wkv0, rw):
    mu, w0, decay_up, a0, aaa_up, gate_up, k_k, k_a, r_k, ln_w, ln_b = rw
    n_chunks = t_len // chunk
    blk0 = row0 // chunk
    row = lambda a: a.reshape(1, -1).astype(F32)
    hsum = jnp.asarray((np.arange(WIDTH)[:, None] // HEAD_DIM == np.arange(WIDTH)[None, :] // HEAD_DIM)
                       .astype(np.float32), dtype=BF16)
    consts = [row(mu), row(w0), decay_up, row(a0), aaa_up, gate_up, row(k_k), row(k_a),
              row(r_k), row(ln_w), row(ln_b), hsum]
    const = lambda a: pl.BlockSpec(a.shape, lambda b, c: (0,) * a.ndim)
    shift3 = shift0.reshape(n_seq, 1, RW_PROJ)
    return pl.pallas_call(
        functools.partial(_rwkv_kernel, chunk=chunk),
        grid=(n_seq, n_chunks),
        in_specs=[pl.BlockSpec((chunk, RW_PROJ), lambda b, c: (blk0 + b * n_chunks + c, 0)),
                  pl.BlockSpec((1, 1, RW_PROJ), lambda b, c: (b, 0, 0)),
                  pl.BlockSpec((1, HEADS, HEAD_DIM, HEAD_DIM), lambda b, c: (b, 0, 0, 0))]
                 + [const(a) for a in consts],
        out_specs=[pl.BlockSpec((chunk, WIDTH), lambda b, c: (b * n_chunks + c, 0)),
                   pl.BlockSpec((1, HEADS, HEAD_DIM, HEAD_DIM), lambda b, c: (b, 0, 0, 0))],
        out_shape=[jax.ShapeDtypeStruct((n_seq * t_len, WIDTH), BF16),
                   jax.ShapeDtypeStruct((n_seq, HEADS, HEAD_DIM, HEAD_DIM), F32)],
        scratch_shapes=[pltpu.VMEM((HEADS, HEAD_DIM, HEAD_DIM), F32), pltpu.VMEM((1, RW_PROJ), F32)],
        compiler_params=pltpu.CompilerParams(dimension_semantics=("arbitrary", "arbitrary"),
                                             vmem_limit_bytes=VMEM_LIMIT),
        name=f"rwkv_chunk{chunk}",
    )(p_all, shift3, wkv0, *consts)


def _merge_kernel(x_ref, attn_ref, rw_ref, gate_ref, wb0_ref, wb1_ref, wo_ref, nf_ref, wr_ref, br_ref,
                  x1_ref, h2_ref, idx_ref, top_ref):
    gates = gate_ref[...]
    merged = (gates[:, :D_MODEL] * _dot(attn_ref[...], wb0_ref[...])
              + gates[:, D_MODEL:] * _dot(rw_ref[...], wb1_ref[...]))
    x1 = x_ref[...] + _dot(merged.astype(BF16), wo_ref[...])
    x1_ref[...] = x1
    h2 = _rms(x1, nf_ref[...])
    h2_ref[...] = h2.astype(BF16)
    logits = _dotf(h2, wr_ref[...]) + br_ref[...]
    lane = lax.broadcasted_iota(jnp.int32, logits.shape, 1)
    vals, idxs = [], []
    for _ in range(TOP_K):
        best = jnp.max(logits, axis=-1, keepdims=True)
        where = jnp.min(jnp.where(logits == best, lane, N_EXPERTS), axis=-1, keepdims=True)
        vals.append(best)
        idxs.append(where)
        logits = jnp.where(lane == where, -jnp.inf, logits)
    top = jnp.concatenate(vals, axis=1)
    e = jnp.exp(top - top[:, 0:1])
    top_ref[...] = e / jnp.sum(e, axis=-1, keepdims=True)
    idx_ref[...] = jnp.concatenate(idxs, axis=1)


def _merge_and_route(x, attn, rw_out, gates, w_branch, w_out, norm_ffn, w_router, b_router):
    rows = x.shape[0]
    tm = ROW_TILE
    wb0 = w_branch[0].astype(BF16)
    wb1 = w_branch[1].astype(BF16)
    wo = w_out.astype(BF16)
    nf = norm_ffn.reshape(1, D_MODEL)
    br = b_router.reshape(1, N_EXPERTS).astype(F32)
    wr = w_router.astype(F32)
    row_spec = lambda n: pl.BlockSpec((tm, n), lambda i: (i, 0))
    full = lambda a: pl.BlockSpec(a.shape, lambda i: (0,) * a.ndim)
    return pl.pallas_call(
        _merge_kernel,
        grid=(rows // tm,),
        in_specs=[row_spec(D_MODEL), row_spec(WIDTH), row_spec(WIDTH), row_spec(2 * D_MODEL),
                  full(wb0), full(wb1), full(wo), full(nf), full(wr), full(br)],
        out_specs=[row_spec(D_MODEL), row_spec(D_MODEL), row_spec(TOP_K), row_spec(TOP_K)],
        out_shape=[jax.ShapeDtypeStruct((rows, D_MODEL), F32),
                   jax.ShapeDtypeStruct((rows, D_MODEL), BF16),
                   jax.ShapeDtypeStruct((rows, TOP_K), jnp.int32),
                   jax.ShapeDtypeStruct((rows, TOP_K), F32)],
        compiler_params=pltpu.CompilerParams(dimension_semantics=("arbitrary",),
                                             vmem_limit_bytes=VMEM_LIMIT),
        name="merge_route",
    )(x, attn, rw_out, gates, wb0, wb1, wo, nf, wr, br)


def _expert_kernel(be_ref, first_ref, used_ref, hs_ref, w1_ref, b1_ref, w2_ref, b2_ref, ys_ref,
                   w1b_ref, w2b_ref):
    i = pl.program_id(0)
    d_expert = w2_ref.shape[1]

    @pl.when(first_ref[i] == 1)
    def _():
        w1b_ref[...] = w1_ref[0].astype(BF16)
        w2b_ref[...] = w2_ref[0].astype(BF16)

    @pl.when(i < used_ref[0])
    def _():
        gu = _dot(hs_ref[...], w1b_ref[...]) + b1_ref[0]
        gt = jnp.minimum(gu[:, :d_expert], SWIGLU_LIMIT)
        up = jnp.clip(gu[:, d_expert:], -SWIGLU_LIMIT, SWIGLU_LIMIT)
        act = (up + 1.0) * (gt * jax.nn.sigmoid(SWIGLU_ALPHA * gt))
        ys_ref[...] = _dot(act.astype(BF16), w2b_ref[...]) + b2_ref[0]

    @pl.when(i >= used_ref[0])
    def _():
        ys_ref[...] = jnp.zeros_like(ys_ref)


def _experts(hs, block_e, first, used, w1, b1, w2, b2):
    tmb = MOE_TILE
    n_blocks = hs.shape[0] // tmb
    d_expert = w2.shape[1]
    b1r = b1.reshape(N_EXPERTS, 1, 2 * d_expert)
    b2r = b2.reshape(N_EXPERTS, 1, D_MODEL)
    grid_spec = pltpu.PrefetchScalarGridSpec(
        num_scalar_prefetch=3,
        grid=(n_blocks,),
        in_specs=[pl.BlockSpec((tmb, D_MODEL), lambda i, be, fi, us: (i, 0)),
                  pl.BlockSpec((1, D_MODEL, 2 * d_expert), lambda i, be, fi, us: (be[i], 0, 0)),
                  pl.BlockSpec((1, 1, 2 * d_expert), lambda i, be, fi, us: (be[i], 0, 0)),
                  pl.BlockSpec((1, d_expert, D_MODEL), lambda i, be, fi, us: (be[i], 0, 0)),
                  pl.BlockSpec((1, 1, D_MODEL), lambda i, be, fi, us: (be[i], 0, 0))],
        out_specs=pl.BlockSpec((tmb, D_MODEL), lambda i, be, fi, us: (i, 0)),
        scratch_shapes=[pltpu.VMEM((D_MODEL, 2 * d_expert), BF16), pltpu.VMEM((d_expert, D_MODEL), BF16)],
    )
    return pl.pallas_call(
        _expert_kernel,
        grid_spec=grid_spec,
        out_shape=jax.ShapeDtypeStruct((n_blocks * tmb, D_MODEL), F32),
        compiler_params=pltpu.CompilerParams(dimension_semantics=("arbitrary",),
                                             vmem_limit_bytes=VMEM_LIMIT),
        name="experts",
    )(block_e, first, used, hs, w1, b1r, w2, b2r)


def _route_tables(top_idx, n_rows):
    tmb = MOE_TILE
    nk = n_rows * TOP_K
    n_blocks = nk // tmb + N_EXPERTS
    flat_e = top_idx.reshape(-1)
    flat_tok = jnp.arange(nk, dtype=jnp.int32) // TOP_K
    order = jnp.argsort(flat_e)
    e_sorted = flat_e[order]
    counts = jnp.bincount(flat_e, length=N_EXPERTS)
    padded = (counts + tmb - 1) // tmb * tmb
    pad_end = jnp.cumsum(padded)
    pad_start = pad_end - padded
    start = jnp.cumsum(counts) - counts
    dest = (pad_start[e_sorted] + jnp.arange(nk) - start[e_sorted]).astype(jnp.int32)
    row_tok = jnp.zeros((n_blocks * tmb,), jnp.int32).at[dest].set(flat_tok[order])
    slot_pos = jnp.zeros((nk,), jnp.int32).at[order].set(dest)
    block_start = jnp.arange(n_blocks) * tmb
    block_e = jnp.minimum(jnp.searchsorted(pad_end, block_start, side='right'), N_EXPERTS - 1).astype(jnp.int32)
    first = jnp.concatenate([jnp.ones((1,), jnp.int32),
                             (block_e[1:] != block_e[:-1]).astype(jnp.int32)])
    used = (pad_end[-1] // tmb).astype(jnp.int32).reshape(1)
    return row_tok, slot_pos.reshape(n_rows, TOP_K), block_e, first, used


def _combine_kernel(x1_ref, ys_ref, gate_ref, nf_ref, o_ref):
    acc = x1_ref[...]
    gate = gate_ref[...]
    for j in range(TOP_K):
        acc = acc + gate[:, j:j + 1] * ys_ref[j]
    o_ref[...] = _rms(acc, nf_ref[...])


def _combine(x1, ys_slots, gate, norm_final):
    rows = x1.shape[0]
    tm = ROW_TILE
    nf = norm_final.reshape(1, D_MODEL)
    return pl.pallas_call(
        _combine_kernel,
        grid=(rows // tm,),
        in_specs=[pl.BlockSpec((tm, D_MODEL), lambda i: (i, 0)),
                  pl.BlockSpec((TOP_K, tm, D_MODEL), lambda i: (0, i, 0)),
                  pl.BlockSpec((tm, TOP_K), lambda i: (i, 0)),
                  pl.BlockSpec((1, D_MODEL), lambda i: (0, 0))],
        out_specs=pl.BlockSpec((tm, D_MODEL), lambda i: (i, 0)),
        out_shape=jax.ShapeDtypeStruct((rows, D_MODEL), F32),
        compiler_params=pltpu.CompilerParams(dimension_semantics=("arbitrary",),
                                             vmem_limit_bytes=VMEM_LIMIT),
        name="combine_norm",
    )(x1, ys_slots, gate, nf)


def kernel(x_prompt, x_sample, cache_k, cache_v, state_wkv, state_shift, page_table, meta, norm_mix, w_in, sb_bias, rw_mu, rw_w0, rw_decay_up, rw_a0, rw_aaa_up, rw_gate_up, rw_k_k, rw_k_a, rw_r_k, rw_ln_w, rw_ln_b, w_gate, b_gate, w_branch, w_out, norm_ffn, w_router, b_router, w1, b1, w2, b2, norm_final):
    assert w_in.shape[0] == 1, "single layer"
    batch, seq, _ = x_prompt.shape
    n_seq, n_new, _ = x_sample.shape
    t_real = N_META + seq
    t_pad = -(-t_real // ATT_TILE) * ATT_TILE
    pad = t_pad - t_real
    first_tok = pad + N_META
    assert t_pad % RWKV_CHUNK == 0 and (batch * t_pad) % ROW_TILE == 0 and (n_seq * n_new) % ROW_TILE == 0
    row_s = batch * t_pad

    meta_rows = jnp.broadcast_to(meta.astype(F32)[None], (batch, N_META, D_MODEL))
    xp = jnp.concatenate([jnp.zeros((batch, pad, D_MODEL), F32), meta_rows, x_prompt], axis=1)
    x_all = jnp.concatenate([xp.reshape(row_s, D_MODEL), x_sample.reshape(n_seq * n_new, D_MODEL)], axis=0)

    qb, k_all, v_all, kb, vb, p_all, gates = _projections(x_all, norm_mix[0], w_in[0], w_gate[0], b_gate[0])

    bias = sb_bias[0].astype(F32)
    attn_p = _prompt_attention(qb, kb, vb, bias, batch, t_pad, pad)
    pool_k = cache_k[0].reshape(cache_k.shape[1], cache_k.shape[2], WIDTH)
    pool_v = cache_v[0].reshape(cache_v.shape[1], cache_v.shape[2], WIDTH)
    attn_s = _paged_attention(qb, k_all, v_all, pool_k, pool_v, page_table, bias, row_s, n_seq, n_new)

    rw = (rw_mu[0], rw_w0[0], rw_decay_up[0], rw_a0[0], rw_aaa_up[0], rw_gate_up[0],
          rw_k_k[0], rw_k_a[0], rw_r_k[0], rw_ln_w[0], rw_ln_b[0])
    y_p, wkv_p = _rwkv(p_all, 0, batch, t_pad, RWKV_CHUNK,
                       jnp.zeros((batch, RW_PROJ), F32),
                       jnp.zeros((batch, HEADS, HEAD_DIM, HEAD_DIM), F32), rw)
    y_s, wkv_s = _rwkv(p_all, row_s, n_seq, n_new, n_new, state_shift[0], state_wkv[0], rw)

    attn = jnp.concatenate([attn_p, attn_s], axis=0)
    rw_out = jnp.concatenate([y_p, y_s], axis=0)
    x1, h2, top_idx, top_gate = _merge_and_route(x_all, attn, rw_out, gates, w_branch[0], w_out[0],
                                                 norm_ffn[0], w_router[0], b_router[0])

    n_rows = x_all.shape[0]
    row_tok, slot_pos, block_e, first, used = _route_tables(top_idx, n_rows)
    hs = jnp.take(h2, row_tok, axis=0)
    ys = _experts(hs, block_e, first, used, w1[0], b1[0], w2[0], b2[0])
    ys_slots = jnp.take(ys, slot_pos.T, axis=0)
    y_all = _combine(x1, ys_slots, top_gate, norm_final)

    def prompt_rows(a, lo):
        return a[:row_s].reshape(batch, t_pad, a.shape[-1])[:, lo:]

    y_prompt = prompt_rows(y_all, first_tok)
    y_sample = y_all[row_s:].reshape(n_seq, n_new, D_MODEL)
    k_p = prompt_rows(k_all, pad).reshape(1, batch, t_real, HEADS, HEAD_DIM)
    v_p = prompt_rows(v_all, pad).reshape(1, batch, t_real, HEADS, HEAD_DIM)
    shift_p = prompt_rows(p_all, t_pad - 1).reshape(1, batch, RW_PROJ)
    k_s = k_all[row_s:].reshape(1, n_seq, n_new, HEADS, HEAD_DIM)
    v_s = v_all[row_s:].reshape(1, n_seq, n_new, HEADS, HEAD_DIM)
    shift_s = p_all[row_s:].reshape(n_seq, n_new, RW_PROJ)[:, -1].reshape(1, n_seq, RW_PROJ)
    return (y_prompt, y_sample, k_p, v_p, wkv_p[None], shift_p, k_s, v_s, wkv_s[None], shift_s)
```

```python
import functools
import math

import numpy as np
import jax
import jax.numpy as jnp
from jax import lax
from jax.experimental import pallas as pl
from jax.experimental.pallas import tpu as pltpu

F32 = jnp.float32
BF16 = jnp.bfloat16
HIGHEST = lax.Precision.HIGHEST

D_MODEL = 1024
N_META = 16
HEADS = 8
HEAD_DIM = 64
WIDTH = HEADS * HEAD_DIM
LOG2E = math.log2(math.e)
SB_SCALE2 = LOG2E / math.sqrt(HEAD_DIM)
DECAY_LORA = 64
AAA_LORA = 64
GATE_LORA = 128
RW_PROJ = 3 * WIDTH + DECAY_LORA + AAA_LORA + GATE_LORA
GN_EPS = 64e-5
N_EXPERTS = 32
TOP_K = 4
SWIGLU_LIMIT = 7.0
SWIGLU_ALPHA = 1.702
RMS_EPS = 1e-5

ROW_TILE = 256
ATT_TILE = 256
RWKV_CHUNK = 64
PAGES_PER_STEP = 8
MOE_TILE = 256
VMEM_LIMIT = 56 * 1024 * 1024

NT_DIMS = (((1,), (1,)), ((), ()))
TN_DIMS = (((0,), (0,)), ((), ()))
NN_DIMS = (((1,), (0,)), ((), ()))


def _dot(a, b):
    return jnp.dot(a, b, preferred_element_type=F32)


def _dotf(a, b, dims=NN_DIMS):
    return lax.dot_general(a, b, dims, precision=HIGHEST, preferred_element_type=F32)


def _split2(x):
    hi = x.astype(BF16)
    return hi, (x - hi.astype(F32)).astype(BF16)


def _dot3(a, b, dims=NN_DIMS):
    dg = lambda x, y: lax.dot_general(x, y, dims, preferred_element_type=F32)
    return dg(a[0], b[0]) + dg(a[0], b[1]) + dg(a[1], b[0])


def _split3(x):
    hi = x.astype(BF16)
    r1 = x - hi.astype(F32)
    mid = r1.astype(BF16)
    lo = (r1 - mid.astype(F32)).astype(BF16)
    return hi, mid, lo


def _dot_exact_rhs(x, m):
    hi, mid, lo = _split3(x)
    return _dot(hi, m) + _dot(mid, m) + _dot(lo, m)


def _dot_exact_lhs(m, x):
    hi, mid, lo = _split3(x)
    return _dot(m, hi) + _dot(m, mid) + _dot(m, lo)


def _neg_softplus(z):
    return jnp.minimum(-z, 0.0) - jnp.log1p(jnp.exp(-jnp.abs(z)))


def _rms(x, g):
    return x * lax.rsqrt(jnp.mean(x * x, axis=-1, keepdims=True) + RMS_EPS) * g


def _proj_kernel(x_ref, g_ref, wqkv_ref, wp_ref, wg_ref, bg_ref,
                 q_ref, k_ref, v_ref, kb_ref, vb_ref, p_ref, gate_ref):
    hb = _rms(x_ref[...], g_ref[...]).astype(BF16)
    qkv = _dot(hb, wqkv_ref[...])
    k = qkv[:, WIDTH:2 * WIDTH]
    v = qkv[:, 2 * WIDTH:]
    q_ref[...] = (qkv[:, :WIDTH] * SB_SCALE2).astype(BF16)
    k_ref[...] = k
    v_ref[...] = v
    kb_ref[...] = k.astype(BF16)
    vb_ref[...] = v.astype(BF16)
    p_ref[...] = _dot(hb, wp_ref[...])
    gate_ref[...] = jax.nn.sigmoid(_dot(hb, wg_ref[...]) + bg_ref[...])


def _projections(x, norm_g, w_in, w_gate, b_gate):
    rows = x.shape[0]
    tm = ROW_TILE
    wqkv = w_in[:, :3 * WIDTH].astype(BF16)
    wp = w_in[:, 3 * WIDTH:].astype(BF16)
    wg = w_gate.astype(BF16)
    row_spec = lambda n: pl.BlockSpec((tm, n), lambda i: (i, 0))
    full = lambda a: pl.BlockSpec(a.shape, lambda i: (0,) * a.ndim)
    g2 = norm_g.reshape(1, D_MODEL)
    bg = b_gate.reshape(1, 2 * D_MODEL)
    return pl.pallas_call(
        _proj_kernel,
        grid=(rows // tm,),
        in_specs=[row_spec(D_MODEL), full(g2), full(wqkv), full(wp), full(wg), full(bg)],
        out_specs=[row_spec(WIDTH)] * 5 + [row_spec(RW_PROJ), row_spec(2 * D_MODEL)],
        out_shape=[jax.ShapeDtypeStruct((rows, WIDTH), BF16),
                   jax.ShapeDtypeStruct((rows, WIDTH), F32),
                   jax.ShapeDtypeStruct((rows, WIDTH), F32),
                   jax.ShapeDtypeStruct((rows, WIDTH), BF16),
                   jax.ShapeDtypeStruct((rows, WIDTH), BF16),
                   jax.ShapeDtypeStruct((rows, RW_PROJ), F32),
                   jax.ShapeDtypeStruct((rows, 2 * D_MODEL), F32)],
        compiler_params=pltpu.CompilerParams(dimension_semantics=("arbitrary",),
                                             vmem_limit_bytes=VMEM_LIMIT),
        name="projections",
    )(x, g2, wqkv, wp, wg, bg)


def _stick_logs(z, visible):
    neg_abs = lax.bitcast_convert_type(
        lax.bitcast_convert_type(z, jnp.uint32) | jnp.uint32(0x80000000), F32)
    sp = jnp.maximum(z, 0.0) + jnp.log2(1.0 + jnp.exp2(neg_abs))
    log_beta = z - sp
    if visible is not None:
        sp = jnp.where(visible, sp, 0.0)
    hi = sp.astype(BF16)
    lo = (sp - hi.astype(F32)).astype(BF16)
    return jnp.concatenate([hi, lo], axis=1), log_beta, jnp.sum(sp, axis=-1, keepdims=True)


def _stick_weights(log_beta, after, visible):
    w = jnp.exp2(log_beta - after)
    if visible is not None:
        w = jnp.where(visible, w, 0.0)
    return w


def _suffix_matrix(n):
    j = np.arange(n)
    m = (j[:, None] > j[None, :]).astype(np.float32)
    return jnp.asarray(np.concatenate([m, m], axis=0), dtype=BF16)


def _prompt_attn_kernel(qi_ref, kt_ref, bias_ref, q_ref, k1_ref, v1_ref, k0_ref, v0_ref, m_ref, o_ref,
                        acc_ref, carry_ref, *, tile, pad):
    t = pl.program_id(1)
    i = qi_ref[t]
    j0 = kt_ref[t]
    first = j0 + 1 >= i

    @pl.when(first)
    def _():
        acc_ref[...] = jnp.zeros_like(acc_ref)
        carry_ref[...] = jnp.zeros_like(carry_ref)

    t_far = pl.num_programs(1) * 2 * tile
    subs = ((j0 + 1, k1_ref, v1_ref), (j0, k0_ref, v0_ref))

    def process(masked):
        q, m2 = q_ref[...], m_ref[...]
        head = lambda h: slice(h * HEAD_DIM, (h + 1) * HEAD_DIM)
        items = []
        for j, k_ref, v_ref in subs:
            visible = None
            if masked:
                q_pos = i * tile + lax.broadcasted_iota(jnp.int32, (tile, tile), 0)
                k_pos = j * tile + lax.broadcasted_iota(jnp.int32, (tile, tile), 1)
                visible = jnp.where(k_pos >= pad, k_pos, t_far) < q_pos
            k, v = k_ref[...], v_ref[...]
            items += [(h, k, v, visible) for h in range(HEADS)]
        scores, logs, afters = {}, {}, {}
        outs = [None] * HEADS
        for r in range(len(items) + 2):
            if r < len(items):
                h, k, _, _ = items[r]
                scores[r] = lax.dot_general(q[:, head(h)], k[:, head(h)], NT_DIMS,
                                            preferred_element_type=F32)
            if 0 <= r - 1 < len(items):
                h, _, _, visible = items[r - 1]
                hilo, log_beta, total = _stick_logs(scores.pop(r - 1) + bias_ref[h], visible)
                afters[r - 1] = _dot(hilo, m2)
                logs[r - 1] = (log_beta, total)
            if 0 <= r - 2 < len(items):
                h, _, v, visible = items[r - 2]
                log_beta, total = logs.pop(r - 2)
                w = _stick_weights(log_beta, afters.pop(r - 2), visible)
                carry = carry_ref[h]
                out = jnp.exp2(-carry) * _dot(w.astype(BF16), v[:, head(h)])
                outs[h] = out if outs[h] is None else outs[h] + out
                carry_ref[h] = carry + total
        acc_ref[...] += jnp.concatenate(outs, axis=1)

    needs_mask = jnp.logical_or(first, j0 == 0)
    pl.when(needs_mask)(lambda: process(True))
    pl.when(jnp.logical_not(needs_mask))(lambda: process(False))

    @pl.when(j0 == 0)
    def _():
        o_ref[...] = acc_ref[...].astype(o_ref.dtype)


def _prompt_attention(qb, kb, vb, bias2, batch, t_pad, pad):
    tile = ATT_TILE
    nq = t_pad // tile
    qi = np.concatenate([np.full(i // 2 + 1, i) for i in range(nq)]).astype(np.int32)
    kt = np.concatenate([2 * np.arange(i // 2, -1, -1) for i in range(nq)]).astype(np.int32)
    q_map = lambda b, t, qi, kt: (b * nq + qi[t], 0)
    hi_map = lambda b, t, qi, kt: (b * nq + jnp.minimum(kt[t] + 1, nq - 1), 0)
    lo_map = lambda b, t, qi, kt: (b * nq + kt[t], 0)
    grid_spec = pltpu.PrefetchScalarGridSpec(
        num_scalar_prefetch=2,
        grid=(batch, len(qi)),
        in_specs=[
            pl.BlockSpec(memory_space=pltpu.SMEM),
            pl.BlockSpec((tile, WIDTH), q_map),
            pl.BlockSpec((tile, WIDTH), hi_map),
            pl.BlockSpec((tile, WIDTH), hi_map),
            pl.BlockSpec((tile, WIDTH), lo_map),
            pl.BlockSpec((tile, WIDTH), lo_map),
            pl.BlockSpec((2 * tile, tile), lambda b, t, qi, kt: (0, 0)),
        ],
        out_specs=pl.BlockSpec((tile, WIDTH), q_map),
        scratch_shapes=[pltpu.VMEM((tile, WIDTH), F32), pltpu.VMEM((HEADS, tile, 1), F32)],
    )
    return pl.pallas_call(
        functools.partial(_prompt_attn_kernel, tile=tile, pad=pad),
        grid_spec=grid_spec,
        out_shape=jax.ShapeDtypeStruct((batch * t_pad, WIDTH), BF16),
        compiler_params=pltpu.CompilerParams(dimension_semantics=("arbitrary", "arbitrary"),
                                             vmem_limit_bytes=VMEM_LIMIT),
        name="prompt_attention",
    )(jnp.asarray(qi), jnp.asarray(kt), bias2, qb, kb, vb, kb, vb, _suffix_matrix(tile))


def _paged_attn_kernel(pt_ref, bias_ref, m_ref, q_ref, kn_ref, vn_ref, *rest,
                       n_new, page, pages_per_step):
    page_refs = rest[:2 * pages_per_step]
    o_ref = rest[2 * pages_per_step]
    acc_ref, carry_ref, knew_ref, vnew_ref = rest[2 * pages_per_step + 1:]
    s = pl.program_id(1)
    rows = HEADS * n_new
    head = lambda h: slice(h * HEAD_DIM, (h + 1) * HEAD_DIM)

    def scores(keys_of, dim_major):
        q = q_ref[...]
        dims = NN_DIMS if dim_major else NT_DIMS
        return jnp.concatenate(
            [lax.dot_general(q[:, head(h)], keys_of(h).astype(BF16), dims, preferred_element_type=F32)
             for h in range(HEADS)], axis=0) + bias_ref[...]

    def logs(z, visible):
        hilo, log_beta, total = _stick_logs(z, visible)
        return _dot(hilo, m_ref[...]), log_beta, total

    def accumulate(after, log_beta, total, values_of, dim_major, visible):
        w = _stick_weights(log_beta, after, visible)
        dims = NT_DIMS if dim_major else NN_DIMS
        out = jnp.concatenate(
            [lax.dot_general(w[h * n_new:(h + 1) * n_new].astype(BF16), values_of(h).astype(BF16), dims,
                             preferred_element_type=F32)
             for h in range(HEADS)], axis=0)
        carry = carry_ref[...]
        acc_ref[...] += jnp.exp2(-carry) * out
        carry_ref[...] = carry + total

    @pl.when(s == 0)
    def _():
        acc_ref[...] = jnp.zeros_like(acc_ref)
        carry_ref[...] = jnp.zeros_like(carry_ref)
        knew_ref[...] = jnp.zeros_like(knew_ref)
        vnew_ref[...] = jnp.zeros_like(vnew_ref)
        for h in range(HEADS):
            knew_ref[h, 0:n_new, :] = kn_ref[:, head(h)]
            vnew_ref[h, 0:n_new, :] = vn_ref[:, head(h)]
        t_idx = lax.rem(lax.broadcasted_iota(jnp.int32, (rows, page), 0), n_new)
        c_idx = lax.broadcasted_iota(jnp.int32, (rows, page), 1)
        visible = c_idx < t_idx
        after, log_beta, total = logs(scores(lambda h: knew_ref[h], False), visible)
        accumulate(after, log_beta, total, lambda h: vnew_ref[h], False, visible)

    def of_page(ref):
        return lambda h: ref[0, 0, h]

    n = pages_per_step
    z, lg = {}, {}
    for r in range(n + 2):
        if r < n:
            z[r] = scores(of_page(page_refs[2 * r]), True)
        if 0 <= r - 1 < n:
            lg[r - 1] = logs(z.pop(r - 1), None)
        if 0 <= r - 2 < n:
            accumulate(*lg.pop(r - 2), of_page(page_refs[2 * (r - 2) + 1]), True, None)

    @pl.when(s == pl.num_programs(1) - 1)
    def _():
        acc = acc_ref[...]
        o_ref[...] = jnp.concatenate([acc[h * n_new:(h + 1) * n_new] for h in range(HEADS)],
                                     axis=1).astype(o_ref.dtype)


def _paged_attention(qb, k_new, v_new, k_pool, v_pool, page_table, bias, row0, n_seq, n_new):
    n_pages = page_table.shape[1]
    page = k_pool.shape[2]
    k_pool = jnp.transpose(k_pool, (0, 1, 3, 4, 2))
    v_pool = jnp.transpose(v_pool, (0, 1, 3, 4, 2))
    pps = PAGES_PER_STEP
    steps = n_pages // pps
    rows = HEADS * n_new
    blk0 = row0 // n_new
    bias_col = jnp.repeat(bias.astype(F32), n_new).reshape(rows, 1)
    const = lambda a: pl.BlockSpec(a.shape, lambda b, s, pt: (0,) * a.ndim)
    new_spec = pl.BlockSpec((n_new, WIDTH), lambda b, s, pt: (blk0 + b, 0))

    def page_spec(p):
        return pl.BlockSpec((1, 1, HEADS, HEAD_DIM, page),
                            lambda b, s, pt: (0, pt[b, n_pages - 1 - (s * pps + p)], 0, 0, 0))

    page_specs, page_args = [], []
    for p in range(pps):
        page_specs += [page_spec(p), page_spec(p)]
        page_args += [k_pool, v_pool]
    m = _suffix_matrix(page)
    grid_spec = pltpu.PrefetchScalarGridSpec(
        num_scalar_prefetch=1,
        grid=(n_seq, steps),
        in_specs=[const(bias_col), const(m), new_spec, new_spec, new_spec] + page_specs,
        out_specs=pl.BlockSpec((n_new, WIDTH), lambda b, s, pt: (b, 0)),
        scratch_shapes=[pltpu.VMEM((rows, HEAD_DIM), F32), pltpu.VMEM((rows, 1), F32),
                        pltpu.VMEM((HEADS, page, HEAD_DIM), F32),
                        pltpu.VMEM((HEADS, page, HEAD_DIM), F32)],
    )
    return pl.pallas_call(
        functools.partial(_paged_attn_kernel, n_new=n_new, page=page, pages_per_step=pps),
        grid_spec=grid_spec,
        out_shape=jax.ShapeDtypeStruct((n_seq * n_new, WIDTH), BF16),
        compiler_params=pltpu.CompilerParams(dimension_semantics=("arbitrary", "arbitrary"),
                                             vmem_limit_bytes=VMEM_LIMIT),
        name="paged_attention",
    )(page_table, bias_col, m, qb, k_new, v_new, *page_args)


def _rwkv_kernel(p_ref, shift_ref, s0_ref, mu_ref, w0_ref, dup_ref, a0_ref, aup_ref, gup_ref,
                 kkw_ref, kaw_ref, rk_ref, lnw_ref, lnb_ref, hsum_ref,
                 y_ref, sout_ref, state_ref, prev_ref, *, chunk):
    c = pl.program_id(1)

    @pl.when(c == 0)
    def _():
        for h in range(HEADS):
            state_ref[h] = s0_ref[0, h].T
        prev_ref[...] = shift_ref[0]

    p = p_ref[...]
    row = lax.broadcasted_iota(jnp.int32, p.shape, 0)
    prev = jnp.where(row == 0, prev_ref[...], pltpu.roll(p, 1, 0))
    prev_ref[...] = p[chunk - 1:chunk, :]
    xm = p + (prev - p) * mu_ref[...]

    c1, c2, c3 = WIDTH, 2 * WIDTH, 3 * WIDTH
    c4 = c3 + DECAY_LORA
    c5 = c4 + AAA_LORA
    r, k, v = xm[:, :c1], xm[:, c1:c2], xm[:, c2:c3]
    w_lo, a_lo, g_lo = xm[:, c3:c4], xm[:, c4:c5], xm[:, c5:]
    log_w = _neg_softplus(-(w0_ref[...] + _dotf(jnp.tanh(w_lo), dup_ref[...]))) - 0.5
    log_decay = -jnp.exp(log_w)
    a = jax.nn.sigmoid(a0_ref[...] + _dotf(a_lo, aup_ref[...]))
    g = _dotf(jax.nn.sigmoid(g_lo), gup_ref[...])

    hsum = hsum_ref[...]
    kk = k * kkw_ref[...]
    kk = kk / jnp.maximum(jnp.sqrt(_dot_exact_rhs(kk * kk, hsum)), 1e-12)
    k = k * (1.0 + (a - 1.0) * kaw_ref[...])

    ti = lax.broadcasted_iota(jnp.int32, (chunk, chunk), 0)
    tj = lax.broadcasted_iota(jnp.int32, (chunk, chunk), 1)
    incl = ti >= tj
    strict = ti > tj
    cum = _dot_exact_lhs(jnp.where(incl, 1.0, 0.0).astype(BF16), log_decay)
    p_incl = jnp.exp(cum)
    p_inv = jnp.exp(-cum)
    p_end = p_incl[chunk - 1:chunk, :]
    a_t = -kk * jnp.exp(cum - log_decay)
    r_t = r * p_incl
    b_t = kk * a * p_inv
    k_t = k * p_inv
    b_e = b_t * p_end
    k_e = k_t * p_end
    eye = ti == tj

    heads = range(HEADS)
    per_head = lambda pair: [(pair[0][:, h * HEAD_DIM:(h + 1) * HEAD_DIM],
                              pair[1][:, h * HEAD_DIM:(h + 1) * HEAD_DIM]) for h in heads]
    split_all = lambda xs: [_split2(x) for x in xs]
    a_s, r_s, b_s, k_s, v_s = (per_head(_split2(x)) for x in (a_t, r_t, b_t, k_t, v))
    be_s, ke_s = per_head(_split2(b_e)), per_head(_split2(k_e))
    s0 = [state_ref[h] for h in heads]
    s0_s = split_all(s0)
    n_ab = [jnp.where(strict, _dot3(a_s[h], b_s[h], NT_DIMS), 0.0) for h in heads]
    a_ak = split_all([jnp.where(strict, _dot3(a_s[h], k_s[h], NT_DIMS), 0.0) for h in heads])
    a_rb = split_all([jnp.where(incl, _dot3(r_s[h], b_s[h], NT_DIMS), 0.0) for h in heads])
    a_rk = split_all([jnp.where(incl, _dot3(r_s[h], k_s[h], NT_DIMS), 0.0) for h in heads])
    inv = [jnp.where(eye, 1.0, 0.0) + n for n in n_ab]
    power_s = split_all(n_ab)
    for _ in range(max(int(math.log2(chunk)) - 1, 0)):
        power_s = split_all([_dot3(power_s[h], power_s[h]) for h in heads])
        inv_s = split_all(inv)
        inv = [inv[h] + _dot3(inv_s[h], power_s[h]) for h in heads]
    x_s = split_all([_dot3(a_s[h], s0_s[h]) + _dot3(a_ak[h], v_s[h]) for h in heads])
    inv_s = split_all(inv)
    u_s = split_all([_dot3(inv_s[h], x_s[h]) for h in heads])
    ys = [_dot3(r_s[h], s0_s[h]) + _dot3(a_rb[h], u_s[h]) + _dot3(a_rk[h], v_s[h]) for h in heads]
    diag = (lax.broadcasted_iota(jnp.int32, (HEAD_DIM, HEAD_DIM), 0)
            == lax.broadcasted_iota(jnp.int32, (HEAD_DIM, HEAD_DIM), 1))
    for h in heads:
        pe = p_end[:, h * HEAD_DIM:(h + 1) * HEAD_DIM]
        pe_col = jnp.sum(jnp.where(diag, pe, 0.0), axis=1, keepdims=True)
        state_ref[h] = (pe_col * s0[h] + _dot3(be_s[h], u_s[h], TN_DIMS) + _dot3(ke_s[h], v_s[h], TN_DIMS))
    y = jnp.concatenate(ys, axis=1)

    inv_n = 1.0 / HEAD_DIM
    mean = _dot_exact_rhs(y, hsum) * inv_n
    yc = y - mean
    var = _dot_exact_rhs(yc * yc, hsum) * inv_n
    yn = yc * lax.rsqrt(var + GN_EPS) * lnw_ref[...] + lnb_ref[...]
    bonus = _dot_exact_rhs(r * k * rk_ref[...], hsum) * v
    y_ref[...] = ((yn + bonus) * g).astype(y_ref.dtype)

    @pl.when(c == pl.num_programs(1) - 1)
    def _():
        for h in range(HEADS):
            sout_ref[0, h] = state_ref[h].T


def _rwkv(p_all, row0, n_seq, t_len, chunk, shift0, wkv0, rw):
    mu, w0, decay_up, a0, aaa_up, gate_up, k_k, k_a, r_k, ln_w, ln_b = rw
    n_chunks = t_len // chunk
    blk0 = row0 // chunk
    row = lambda a: a.reshape(1, -1).astype(F32)
    hsum = jnp.asarray((np.arange(WIDTH)[:, None] // HEAD_DIM == np.arange(WIDTH)[None, :] // HEAD_DIM)
                       .astype(np.float32), dtype=BF16)
    consts = [row(mu), row(w0), decay_up, row(a0), aaa_up, gate_up, row(k_k), row(k_a),
              row(r_k), row(ln_w), row(ln_b), hsum]
    const = lambda a: pl.BlockSpec(a.shape, lambda b, c: (0,) * a.ndim)
    shift3 = shift0.reshape(n_seq, 1, RW_PROJ)
    return pl.pallas_call(
        functools.partial(_rwkv_kernel, chunk=chunk),
        grid=(n_seq, n_chunks),
        in_specs=[pl.BlockSpec((chunk, RW_PROJ), lambda b, c: (blk0 + b * n_chunks + c, 0)),
                  pl.BlockSpec((1, 1, RW_PROJ), lambda b, c: (b, 0, 0)),
                  pl.BlockSpec((1, HEADS, HEAD_DIM, HEAD_DIM), lambda b, c: (b, 0, 0, 0))]
                 + [const(a) for a in consts],
        out_specs=[pl.BlockSpec((chunk, WIDTH), lambda b, c: (b * n_chunks + c, 0)),
                   pl.BlockSpec((1, HEADS, HEAD_DIM, HEAD_DIM), lambda b, c: (b, 0, 0, 0))],
        out_shape=[jax.ShapeDtypeStruct((n_seq * t_len, WIDTH), BF16),
                   jax.ShapeDtypeStruct((n_seq, HEADS, HEAD_DIM, HEAD_DIM), F32)],
        scratch_shapes=[pltpu.VMEM((HEADS, HEAD_DIM, HEAD_DIM), F32), pltpu.VMEM((1, RW_PROJ), F32)],
        compiler_params=pltpu.CompilerParams(dimension_semantics=("arbitrary", "arbitrary"),
                                             vmem_limit_bytes=VMEM_LIMIT),
        name=f"rwkv_chunk{chunk}",
    )(p_all, shift3, wkv0, *consts)


def _merge_kernel(x_ref, attn_ref, rw_ref, gate_ref, wb0_ref, wb1_ref, wo_ref, nf_ref, wr_ref, br_ref,
                  x1_ref, h2_ref, idx_ref, top_ref):
    gates = gate_ref[...]
    merged = (gates[:, :D_MODEL] * _dot(attn_ref[...], wb0_ref[...])
              + gates[:, D_MODEL:] * _dot(rw_ref[...], wb1_ref[...]))
    x1 = x_ref[...] + _dot(merged.astype(BF16), wo_ref[...])
    x1_ref[...] = x1
    h2 = _rms(x1, nf_ref[...])
    h2_ref[...] = h2.astype(BF16)
    logits = _dotf(h2, wr_ref[...]) + br_ref[...]
    lane = lax.broadcasted_iota(jnp.int32, logits.shape, 1)
    vals, idxs = [], []
    for _ in range(TOP_K):
        best = jnp.max(logits, axis=-1, keepdims=True)
        where = jnp.min(jnp.where(logits == best, lane, N_EXPERTS), axis=-1, keepdims=True)
        vals.append(best)
        idxs.append(where)
        logits = jnp.where(lane == where, -jnp.inf, logits)
    top = jnp.concatenate(vals, axis=1)
    e = jnp.exp(top - top[:, 0:1])
    top_ref[...] = e / jnp.sum(e, axis=-1, keepdims=True)
    idx_ref[...] = jnp.concatenate(idxs, axis=1)


def _merge_and_route(x, attn, rw_out, gates, w_branch, w_out, norm_ffn, w_router, b_router):
    rows = x.shape[0]
    tm = ROW_TILE
    wb0 = w_branch[0].astype(BF16)
    wb1 = w_branch[1].astype(BF16)
    wo = w_out.astype(BF16)
    nf = norm_ffn.reshape(1, D_MODEL)
    br = b_router.reshape(1, N_EXPERTS).astype(F32)
    wr = w_router.astype(F32)
    row_spec = lambda n: pl.BlockSpec((tm, n), lambda i: (i, 0))
    full = lambda a: pl.BlockSpec(a.shape, lambda i: (0,) * a.ndim)
    return pl.pallas_call(
        _merge_kernel,
        grid=(rows // tm,),
        in_specs=[row_spec(D_MODEL), row_spec(WIDTH), row_spec(WIDTH), row_spec(2 * D_MODEL),
                  full(wb0), full(wb1), full(wo), full(nf), full(wr), full(br)],
        out_specs=[row_spec(D_MODEL), row_spec(D_MODEL), row_spec(TOP_K), row_spec(TOP_K)],
        out_shape=[jax.ShapeDtypeStruct((rows, D_MODEL), F32),
                   jax.ShapeDtypeStruct((rows, D_MODEL), BF16),
                   jax.ShapeDtypeStruct((rows, TOP_K), jnp.int32),
                   jax.ShapeDtypeStruct((rows, TOP_K), F32)],
        compiler_params=pltpu.CompilerParams(dimension_semantics=("arbitrary",),
                                             vmem_limit_bytes=VMEM_LIMIT),
        name="merge_route",
    )(x, attn, rw_out, gates, wb0, wb1, wo, nf, wr, br)


def _expert_kernel(be_ref, first_ref, used_ref, hs_ref, w1_ref, b1_ref, w2_ref, b2_ref, ys_ref,
                   w1b_ref, w2b_ref):
    i = pl.program_id(0)
    d_expert = w2_ref.shape[1]

    @pl.when(first_ref[i] == 1)
    def _():
        w1b_ref[...] = w1_ref[0].astype(BF16)
        w2b_ref[...] = w2_ref[0].astype(BF16)

    @pl.when(i < used_ref[0])
    def _():
        gu = _dot(hs_ref[...], w1b_ref[...]) + b1_ref[0]
        gt = jnp.minimum(gu[:, :d_expert], SWIGLU_LIMIT)
        up = jnp.clip(gu[:, d_expert:], -SWIGLU_LIMIT, SWIGLU_LIMIT)
        act = (up + 1.0) * (gt * jax.nn.sigmoid(SWIGLU_ALPHA * gt))
        ys_ref[...] = _dot(act.astype(BF16), w2b_ref[...]) + b2_ref[0]

    @pl.when(i >= used_ref[0])
    def _():
        ys_ref[...] = jnp.zeros_like(ys_ref)


def _experts(hs, block_e, first, used, w1, b1, w2, b2):
    tmb = MOE_TILE
    n_blocks = hs.shape[0] // tmb
    d_expert = w2.shape[1]
    b1r = b1.reshape(N_EXPERTS, 1, 2 * d_expert)
    b2r = b2.reshape(N_EXPERTS, 1, D_MODEL)
    grid_spec = pltpu.PrefetchScalarGridSpec(
        num_scalar_prefetch=3,
        grid=(n_blocks,),
        in_specs=[pl.BlockSpec((tmb, D_MODEL), lambda i, be, fi, us: (i, 0)),
                  pl.BlockSpec((1, D_MODEL, 2 * d_expert), lambda i, be, fi, us: (be[i], 0, 0)),
                  pl.BlockSpec((1, 1, 2 * d_expert), lambda i, be, fi, us: (be[i], 0, 0)),
                  pl.BlockSpec((1, d_expert, D_MODEL), lambda i, be, fi, us: (be[i], 0, 0)),
                  pl.BlockSpec((1, 1, D_MODEL), lambda i, be, fi, us: (be[i], 0, 0))],
        out_specs=pl.BlockSpec((tmb, D_MODEL), lambda i, be, fi, us: (i, 0)),
        scratch_shapes=[pltpu.VMEM((D_MODEL, 2 * d_expert), BF16), pltpu.VMEM((d_expert, D_MODEL), BF16)],
    )
    return pl.pallas_call(
        _expert_kernel,
        grid_spec=grid_spec,
        out_shape=jax.ShapeDtypeStruct((n_blocks * tmb, D_MODEL), F32),
        compiler_params=pltpu.CompilerParams(dimension_semantics=("arbitrary",),
                                             vmem_limit_bytes=VMEM_LIMIT),
        name="experts",
    )(block_e, first, used, hs, w1, b1r, w2, b2r)


def _route_tables(top_idx, n_rows):
    tmb = MOE_TILE
    nk = n_rows * TOP_K
    n_blocks = nk // tmb + N_EXPERTS
    flat_e = top_idx.reshape(-1)
    flat_tok = jnp.arange(nk, dtype=jnp.int32) // TOP_K
    order = jnp.argsort(flat_e)
    e_sorted = flat_e[order]
    counts = jnp.bincount(flat_e, length=N_EXPERTS)
    padded = (counts + tmb - 1) // tmb * tmb
    pad_end = jnp.cumsum(padded)
    pad_start = pad_end - padded
    start = jnp.cumsum(counts) - counts
    dest = (pad_start[e_sorted] + jnp.arange(nk) - start[e_sorted]).astype(jnp.int32)
    row_tok = jnp.zeros((n_blocks * tmb,), jnp.int32).at[dest].set(flat_tok[order])
    slot_pos = jnp.zeros((nk,), jnp.int32).at[order].set(dest)
    block_start = jnp.arange(n_blocks) * tmb
    block_e = jnp.minimum(jnp.searchsorted(pad_end, block_start, side='right'), N_EXPERTS - 1).astype(jnp.int32)
    first = jnp.concatenate([jnp.ones((1,), jnp.int32),
                             (block_e[1:] != block_e[:-1]).astype(jnp.int32)])
    used = (pad_end[-1] // tmb).astype(jnp.int32).reshape(1)
    return row_tok, slot_pos.reshape(n_rows, TOP_K), block_e, first, used


def _combine_kernel(x1_ref, ys_ref, gate_ref, nf_ref, o_ref):
    acc = x1_ref[...]
    gate = gate_ref[...]
    for j in range(TOP_K):
        acc = acc + gate[:, j:j + 1] * ys_ref[j]
    o_ref[...] = _rms(acc, nf_ref[...])


def _combine(x1, ys_slots, gate, norm_final):
    rows = x1.shape[0]
    tm = ROW_TILE
    nf = norm_final.reshape(1, D_MODEL)
    return pl.pallas_call(
        _combine_kernel,
        grid=(rows // tm,),
        in_specs=[pl.BlockSpec((tm, D_MODEL), lambda i: (i, 0)),
                  pl.BlockSpec((TOP_K, tm, D_MODEL), lambda i: (0, i, 0)),
                  pl.BlockSpec((tm, TOP_K), lambda i: (i, 0)),
                  pl.BlockSpec((1, D_MODEL), lambda i: (0, 0))],
        out_specs=pl.BlockSpec((tm, D_MODEL), lambda i: (i, 0)),
        out_shape=jax.ShapeDtypeStruct((rows, D_MODEL), F32),
        compiler_params=pltpu.CompilerParams(dimension_semantics=("arbitrary",),
                                             vmem_limit_bytes=VMEM_LIMIT),
        name="combine_norm",
    )(x1, ys_slots, gate, nf)


def kernel(x_prompt, x_sample, cache_k, cache_v, state_wkv, state_shift, page_table, meta, norm_mix, w_in, sb_bias, rw_mu, rw_w0, rw_decay_up, rw_a0, rw_aaa_up, rw_gate_up, rw_k_k, rw_k_a, rw_r_k, rw_ln_w, rw_ln_b, w_gate, b_gate, w_branch, w_out, norm_ffn, w_router, b_router, w1, b1, w2, b2, norm_final):
    assert w_in.shape[0] == 1, "single layer"
    batch, seq, _ = x_prompt.shape
    n_seq, n_new, _ = x_sample.shape
    t_real = N_META + seq
    t_pad = -(-t_real // ATT_TILE) * ATT_TILE
    pad = t_pad - t_real
    first_tok = pad + N_META
    assert t_pad % RWKV_CHUNK == 0 and (batch * t_pad) % ROW_TILE == 0 and (n_seq * n_new) % ROW_TILE == 0
    row_s = batch * t_pad

    meta_rows = jnp.broadcast_to(meta.astype(F32)[None], (batch, N_META, D_MODEL))
    xp = jnp.concatenate([jnp.zeros((batch, pad, D_MODEL), F32), meta_rows, x_prompt], axis=1)
    x_all = jnp.concatenate([xp.reshape(row_s, D_MODEL), x_sample.reshape(n_seq * n_new, D_MODEL)], axis=0)

    qb, k_all, v_all, kb, vb, p_all, gates = _projections(x_all, norm_mix[0], w_in[0], w_gate[0], b_gate[0])

    bias = sb_bias[0].astype(F32) * LOG2E
    attn_p = _prompt_attention(qb, kb, vb, bias, batch, t_pad, pad)
    attn_s = _paged_attention(qb, k_all, v_all, cache_k, cache_v, page_table, bias, row_s, n_seq, n_new)

    rw = (rw_mu[0], rw_w0[0], rw_decay_up[0], rw_a0[0], rw_aaa_up[0], rw_gate_up[0],
          rw_k_k[0], rw_k_a[0], rw_r_k[0], rw_ln_w[0], rw_ln_b[0])
    y_p, wkv_p = _rwkv(p_all, 0, batch, t_pad, RWKV_CHUNK,
                       jnp.zeros((batch, RW_PROJ), F32),
                       jnp.zeros((batch, HEADS, HEAD_DIM, HEAD_DIM), F32), rw)
    y_s, wkv_s = _rwkv(p_all, row_s, n_seq, n_new, n_new, state_shift[0], state_wkv[0], rw)

    attn = jnp.concatenate([attn_p, attn_s], axis=0)
    rw_out = jnp.concatenate([y_p, y_s], axis=0)
    x1, h2, top_idx, top_gate = _merge_and_route(x_all, attn, rw_out, gates, w_branch[0], w_out[0],
                                                 norm_ffn[0], w_router[0], b_router[0])

    n_rows = x_all.shape[0]
    row_tok, slot_pos, block_e, first, used = _route_tables(top_idx, n_rows)
    hs = jnp.take(h2, row_tok, axis=0)
    ys = _experts(hs, block_e, first, used, w1[0], b1[0], w2[0], b2[0])
    ys_slots = jnp.take(ys, slot_pos.T, axis=0)
    y_all = _combine(x1, ys_slots, top_gate, norm_final)

    def prompt_rows(a, lo):
        return a[:row_s].reshape(batch, t_pad, a.shape[-1])[:, lo:]

    y_prompt = prompt_rows(y_all, first_tok)
    y_sample = y_all[row_s:].reshape(n_seq, n_new, D_MODEL)
    k_p = prompt_rows(k_all, pad).reshape(1, batch, t_real, HEADS, HEAD_DIM)
    v_p = prompt_rows(v_all, pad).reshape(1, batch, t_real, HEADS, HEAD_DIM)
    shift_p = prompt_rows(p_all, t_pad - 1).reshape(1, batch, RW_PROJ)
    k_s = k_all[row_s:].reshape(1, n_seq, n_new, HEADS, HEAD_DIM)
    v_s = v_all[row_s:].reshape(1, n_seq, n_new, HEADS, HEAD_DIM)
    shift_s = p_all[row_s:].reshape(n_seq, n_new, RW_PROJ)[:, -1].reshape(1, n_seq, RW_PROJ)
    return (y_prompt, y_sample, k_p, v_p, wkv_p[None], shift_p, k_s, v_s, wkv_s[None], shift_s)
```

```python
import functools
import math

import numpy as np
import jax
import jax.numpy as jnp
from jax import lax
from jax.experimental import pallas as pl
from jax.experimental.pallas import tpu as pltpu
from jax.experimental.pallas import tpu_sc as plsc

F32 = jnp.float32
BF16 = jnp.bfloat16
HIGHEST = lax.Precision.HIGHEST

D_MODEL = 1024
N_META = 16
HEADS = 8
HEAD_DIM = 64
WIDTH = HEADS * HEAD_DIM
LOG2E = math.log2(math.e)
SB_SCALE2 = LOG2E / math.sqrt(HEAD_DIM)
DECAY_LORA = 64
AAA_LORA = 64
GATE_LORA = 128
RW_PROJ = 3 * WIDTH + DECAY_LORA + AAA_LORA + GATE_LORA
GN_EPS = 64e-5
N_EXPERTS = 32
TOP_K = 4
SWIGLU_LIMIT = 7.0
SWIGLU_ALPHA = 1.702
RMS_EPS = 1e-5

ROW_TILE = 256
ATT_TILE = 256
RWKV_CHUNK = 64
PAGES_PER_STEP = 8
MOE_TILE = 256
VMEM_LIMIT = 56 * 1024 * 1024

NT_DIMS = (((1,), (1,)), ((), ()))
TN_DIMS = (((0,), (0,)), ((), ()))
NN_DIMS = (((1,), (0,)), ((), ()))


def _dot(a, b):
    return jnp.dot(a, b, preferred_element_type=F32)


def _dotf(a, b, dims=NN_DIMS):
    return lax.dot_general(a, b, dims, precision=HIGHEST, preferred_element_type=F32)


def _split2(x):
    hi = x.astype(BF16)
    return hi, (x - hi.astype(F32)).astype(BF16)


def _dot3(a, b, dims=NN_DIMS):
    dg = lambda x, y: lax.dot_general(x, y, dims, preferred_element_type=F32)
    return dg(a[0], b[0]) + dg(a[0], b[1]) + dg(a[1], b[0])


def _split3(x):
    hi = x.astype(BF16)
    r1 = x - hi.astype(F32)
    mid = r1.astype(BF16)
    lo = (r1 - mid.astype(F32)).astype(BF16)
    return hi, mid, lo


def _dot_exact_rhs(x, m):
    hi, mid, lo = _split3(x)
    return _dot(hi, m) + _dot(mid, m) + _dot(lo, m)


def _dot_exact_lhs(m, x):
    hi, mid, lo = _split3(x)
    return _dot(m, hi) + _dot(m, mid) + _dot(m, lo)


def _neg_softplus(z):
    return jnp.minimum(-z, 0.0) - jnp.log1p(jnp.exp(-jnp.abs(z)))


def _rms(x, g):
    return x * lax.rsqrt(jnp.mean(x * x, axis=-1, keepdims=True) + RMS_EPS) * g


def _proj_kernel(x_ref, g_ref, wqkv_ref, wp_ref, wg_ref, bg_ref,
                 q_ref, k_ref, v_ref, kb_ref, vb_ref, p_ref, gate_ref):
    hb = _rms(x_ref[...], g_ref[...]).astype(BF16)
    qkv = _dot(hb, wqkv_ref[...])
    k = qkv[:, WIDTH:2 * WIDTH]
    v = qkv[:, 2 * WIDTH:]
    q_ref[...] = (qkv[:, :WIDTH] * SB_SCALE2).astype(BF16)
    k_ref[...] = k
    v_ref[...] = v
    kb_ref[...] = k.astype(BF16)
    vb_ref[...] = v.astype(BF16)
    p_ref[...] = _dot(hb, wp_ref[...])
    gate_ref[...] = jax.nn.sigmoid(_dot(hb, wg_ref[...]) + bg_ref[...])


def _projections(x, norm_g, w_in, w_gate, b_gate):
    rows = x.shape[0]
    tm = ROW_TILE
    wqkv = w_in[:, :3 * WIDTH].astype(BF16)
    wp = w_in[:, 3 * WIDTH:].astype(BF16)
    wg = w_gate.astype(BF16)
    row_spec = lambda n: pl.BlockSpec((tm, n), lambda i: (i, 0))
    full = lambda a: pl.BlockSpec(a.shape, lambda i: (0,) * a.ndim)
    g2 = norm_g.reshape(1, D_MODEL)
    bg = b_gate.reshape(1, 2 * D_MODEL)
    return pl.pallas_call(
        _proj_kernel,
        grid=(rows // tm,),
        in_specs=[row_spec(D_MODEL), full(g2), full(wqkv), full(wp), full(wg), full(bg)],
        out_specs=[row_spec(WIDTH)] * 5 + [row_spec(RW_PROJ), row_spec(2 * D_MODEL)],
        out_shape=[jax.ShapeDtypeStruct((rows, WIDTH), BF16),
                   jax.ShapeDtypeStruct((rows, WIDTH), F32),
                   jax.ShapeDtypeStruct((rows, WIDTH), F32),
                   jax.ShapeDtypeStruct((rows, WIDTH), BF16),
                   jax.ShapeDtypeStruct((rows, WIDTH), BF16),
                   jax.ShapeDtypeStruct((rows, RW_PROJ), F32),
                   jax.ShapeDtypeStruct((rows, 2 * D_MODEL), F32)],
        compiler_params=pltpu.CompilerParams(dimension_semantics=("arbitrary",),
                                             vmem_limit_bytes=VMEM_LIMIT),
        name="projections",
    )(x, g2, wqkv, wp, wg, bg)


def _stick_logs(z, visible):
    neg_abs = lax.bitcast_convert_type(
        lax.bitcast_convert_type(z, jnp.uint32) | jnp.uint32(0x80000000), F32)
    sp = jnp.maximum(z, 0.0) + jnp.log2(1.0 + jnp.exp2(neg_abs))
    log_beta = z - sp
    if visible is not None:
        sp = jnp.where(visible, sp, 0.0)
    hi = sp.astype(BF16)
    lo = (sp - hi.astype(F32)).astype(BF16)
    return jnp.concatenate([hi, lo], axis=1), log_beta, jnp.sum(sp, axis=-1, keepdims=True)


def _stick_weights(log_beta, after, visible):
    w = jnp.exp2(log_beta - after)
    if visible is not None:
        w = jnp.where(visible, w, 0.0)
    return w


def _suffix_matrix(n):
    j = np.arange(n)
    m = (j[:, None] > j[None, :]).astype(np.float32)
    return jnp.asarray(np.concatenate([m, m], axis=0), dtype=BF16)


def _prompt_attn_kernel(qi_ref, kt_ref, bias_ref, q_ref, k1_ref, v1_ref, k0_ref, v0_ref, m_ref, o_ref,
                        acc_ref, carry_ref, *, tile, pad):
    t = pl.program_id(1)
    i = qi_ref[t]
    j0 = kt_ref[t]
    first = j0 + 1 >= i

    @pl.when(first)
    def _():
        acc_ref[...] = jnp.zeros_like(acc_ref)
        carry_ref[...] = jnp.zeros_like(carry_ref)

    t_far = pl.num_programs(1) * 2 * tile
    subs = ((j0 + 1, k1_ref, v1_ref), (j0, k0_ref, v0_ref))

    def process(masked):
        q, m2 = q_ref[...], m_ref[...]
        head = lambda h: slice(h * HEAD_DIM, (h + 1) * HEAD_DIM)
        items = []
        for j, k_ref, v_ref in subs:
            visible = None
            if masked:
                q_pos = i * tile + lax.broadcasted_iota(jnp.int32, (tile, tile), 0)
                k_pos = j * tile + lax.broadcasted_iota(jnp.int32, (tile, tile), 1)
                visible = jnp.where(k_pos >= pad, k_pos, t_far) < q_pos
            k, v = k_ref[...], v_ref[...]
            items += [(h, k, v, visible) for h in range(HEADS)]
        scores, logs, afters = {}, {}, {}
        outs = [None] * HEADS
        for r in range(len(items) + 2):
            if r < len(items):
                h, k, _, _ = items[r]
                scores[r] = lax.dot_general(q[:, head(h)], k[:, head(h)], NT_DIMS,
                                            preferred_element_type=F32)
            if 0 <= r - 1 < len(items):
                h, _, _, visible = items[r - 1]
                hilo, log_beta, total = _stick_logs(scores.pop(r - 1) + bias_ref[h], visible)
                afters[r - 1] = _dot(hilo, m2)
                logs[r - 1] = (log_beta, total)
            if 0 <= r - 2 < len(items):
                h, _, v, visible = items[r - 2]
                log_beta, total = logs.pop(r - 2)
                w = _stick_weights(log_beta, afters.pop(r - 2), visible)
                carry = carry_ref[h]
                out = jnp.exp2(-carry) * _dot(w.astype(BF16), v[:, head(h)])
                outs[h] = out if outs[h] is None else outs[h] + out
                carry_ref[h] = carry + total
        acc_ref[...] += jnp.concatenate(outs, axis=1)

    needs_mask = jnp.logical_or(first, j0 == 0)
    pl.when(needs_mask)(lambda: process(True))
    pl.when(jnp.logical_not(needs_mask))(lambda: process(False))

    @pl.when(j0 == 0)
    def _():
        o_ref[...] = acc_ref[...].astype(o_ref.dtype)


def _prompt_attention(qb, kb, vb, bias2, batch, t_pad, pad):
    tile = ATT_TILE
    nq = t_pad // tile
    qi = np.concatenate([np.full(i // 2 + 1, i) for i in range(nq)]).astype(np.int32)
    kt = np.concatenate([2 * np.arange(i // 2, -1, -1) for i in range(nq)]).astype(np.int32)
    q_map = lambda b, t, qi, kt: (b * nq + qi[t], 0)
    hi_map = lambda b, t, qi, kt: (b * nq + jnp.minimum(kt[t] + 1, nq - 1), 0)
    lo_map = lambda b, t, qi, kt: (b * nq + kt[t], 0)
    grid_spec = pltpu.PrefetchScalarGridSpec(
        num_scalar_prefetch=2,
        grid=(batch, len(qi)),
        in_specs=[
            pl.BlockSpec(memory_space=pltpu.SMEM),
            pl.BlockSpec((tile, WIDTH), q_map),
            pl.BlockSpec((tile, WIDTH), hi_map),
            pl.BlockSpec((tile, WIDTH), hi_map),
            pl.BlockSpec((tile, WIDTH), lo_map),
            pl.BlockSpec((tile, WIDTH), lo_map),
            pl.BlockSpec((2 * tile, tile), lambda b, t, qi, kt: (0, 0)),
        ],
        out_specs=pl.BlockSpec((tile, WIDTH), q_map),
        scratch_shapes=[pltpu.VMEM((tile, WIDTH), F32), pltpu.VMEM((HEADS, tile, 1), F32)],
    )
    return pl.pallas_call(
        functools.partial(_prompt_attn_kernel, tile=tile, pad=pad),
        grid_spec=grid_spec,
        out_shape=jax.ShapeDtypeStruct((batch * t_pad, WIDTH), BF16),
        compiler_params=pltpu.CompilerParams(dimension_semantics=("arbitrary", "arbitrary"),
                                             vmem_limit_bytes=VMEM_LIMIT),
        name="prompt_attention",
    )(jnp.asarray(qi), jnp.asarray(kt), bias2, qb, kb, vb, kb, vb, _suffix_matrix(tile))


def _paged_attn_kernel(pt_ref, bias_ref, m_ref, q_ref, kn_ref, vn_ref, *rest,
                       n_new, page, pages_per_step):
    page_refs = rest[:2 * pages_per_step]
    o_ref = rest[2 * pages_per_step]
    acc_ref, carry_ref, knew_ref, vnew_ref = rest[2 * pages_per_step + 1:]
    s = pl.program_id(1)
    rows = HEADS * n_new
    head = lambda h: slice(h * HEAD_DIM, (h + 1) * HEAD_DIM)

    def scores(keys_of, dim_major):
        q = q_ref[...]
        dims = NN_DIMS if dim_major else NT_DIMS
        return jnp.concatenate(
            [lax.dot_general(q[:, head(h)], keys_of(h).astype(BF16), dims, preferred_element_type=F32)
             for h in range(HEADS)], axis=0) + bias_ref[...]

    def logs(z, visible):
        hilo, log_beta, total = _stick_logs(z, visible)
        return _dot(hilo, m_ref[...]), log_beta, total

    def accumulate(after, log_beta, total, values_of, dim_major, visible):
        w = _stick_weights(log_beta, after, visible)
        dims = NT_DIMS if dim_major else NN_DIMS
        out = jnp.concatenate(
            [lax.dot_general(w[h * n_new:(h + 1) * n_new].astype(BF16), values_of(h).astype(BF16), dims,
                             preferred_element_type=F32)
             for h in range(HEADS)], axis=0)
        carry = carry_ref[...]
        acc_ref[...] += jnp.exp2(-carry) * out
        carry_ref[...] = carry + total

    @pl.when(s == 0)
    def _():
        acc_ref[...] = jnp.zeros_like(acc_ref)
        carry_ref[...] = jnp.zeros_like(carry_ref)
        knew_ref[...] = jnp.zeros_like(knew_ref)
        vnew_ref[...] = jnp.zeros_like(vnew_ref)
        for h in range(HEADS):
            knew_ref[h, 0:n_new, :] = kn_ref[:, head(h)]
            vnew_ref[h, 0:n_new, :] = vn_ref[:, head(h)]
        t_idx = lax.rem(lax.broadcasted_iota(jnp.int32, (rows, page), 0), n_new)
        c_idx = lax.broadcasted_iota(jnp.int32, (rows, page), 1)
        visible = c_idx < t_idx
        after, log_beta, total = logs(scores(lambda h: knew_ref[h], False), visible)
        accumulate(after, log_beta, total, lambda h: vnew_ref[h], False, visible)

    def of_page(ref):
        return lambda h: ref[0, 0, h]

    n = pages_per_step
    z, lg = {}, {}
    for r in range(n + 2):
        if r < n:
            z[r] = scores(of_page(page_refs[2 * r]), True)
        if 0 <= r - 1 < n:
            lg[r - 1] = logs(z.pop(r - 1), None)
        if 0 <= r - 2 < n:
            accumulate(*lg.pop(r - 2), of_page(page_refs[2 * (r - 2) + 1]), True, None)

    @pl.when(s == pl.num_programs(1) - 1)
    def _():
        acc = acc_ref[...]
        o_ref[...] = jnp.concatenate([acc[h * n_new:(h + 1) * n_new] for h in range(HEADS)],
                                     axis=1).astype(o_ref.dtype)


def _paged_attention(qb, k_new, v_new, k_pool, v_pool, page_table, bias, row0, n_seq, n_new):
    n_pages = page_table.shape[1]
    page = k_pool.shape[2]
    k_pool = jnp.transpose(k_pool, (0, 1, 3, 4, 2))
    v_pool = jnp.transpose(v_pool, (0, 1, 3, 4, 2))
    pps = PAGES_PER_STEP
    steps = n_pages // pps
    rows = HEADS * n_new
    blk0 = row0 // n_new
    bias_col = jnp.repeat(bias.astype(F32), n_new).reshape(rows, 1)
    const = lambda a: pl.BlockSpec(a.shape, lambda b, s, pt: (0,) * a.ndim)
    new_spec = pl.BlockSpec((n_new, WIDTH), lambda b, s, pt: (blk0 + b, 0))

    def page_spec(p):
        return pl.BlockSpec((1, 1, HEADS, HEAD_DIM, page),
                            lambda b, s, pt: (0, pt[b, n_pages - 1 - (s * pps + p)], 0, 0, 0))

    page_specs, page_args = [], []
    for p in range(pps):
        page_specs += [page_spec(p), page_spec(p)]
        page_args += [k_pool, v_pool]
    m = _suffix_matrix(page)
    grid_spec = pltpu.PrefetchScalarGridSpec(
        num_scalar_prefetch=1,
        grid=(n_seq, steps),
        in_specs=[const(bias_col), const(m), new_spec, new_spec, new_spec] + page_specs,
        out_specs=pl.BlockSpec((n_new, WIDTH), lambda b, s, pt: (b, 0)),
        scratch_shapes=[pltpu.VMEM((rows, HEAD_DIM), F32), pltpu.VMEM((rows, 1), F32),
                        pltpu.VMEM((HEADS, page, HEAD_DIM), F32),
                        pltpu.VMEM((HEADS, page, HEAD_DIM), F32)],
    )
    return pl.pallas_call(
        functools.partial(_paged_attn_kernel, n_new=n_new, page=page, pages_per_step=pps),
        grid_spec=grid_spec,
        out_shape=jax.ShapeDtypeStruct((n_seq * n_new, WIDTH), BF16),
        compiler_params=pltpu.CompilerParams(dimension_semantics=("arbitrary", "arbitrary"),
                                             vmem_limit_bytes=VMEM_LIMIT),
        name="paged_attention",
    )(page_table, bias_col, m, qb, k_new, v_new, *page_args)


def _rwkv_kernel(p_ref, shift_ref, s0_ref, mu_ref, w0_ref, dup_ref, a0_ref, aup_ref, gup_ref,
                 kkw_ref, kaw_ref, rk_ref, lnw_ref, lnb_ref, hsum_ref,
                 y_ref, sout_ref, state_ref, prev_ref, *, chunk):
    c = pl.program_id(1)

    @pl.when(c == 0)
    def _():
        for h in range(HEADS):
            state_ref[h] = s0_ref[0, h].T
        prev_ref[...] = shift_ref[0]

    p = p_ref[...]
    row = lax.broadcasted_iota(jnp.int32, p.shape, 0)
    prev = jnp.where(row == 0, prev_ref[...], pltpu.roll(p, 1, 0))
    prev_ref[...] = p[chunk - 1:chunk, :]
    xm = p + (prev - p) * mu_ref[...]

    c1, c2, c3 = WIDTH, 2 * WIDTH, 3 * WIDTH
    c4 = c3 + DECAY_LORA
    c5 = c4 + AAA_LORA
    r, k, v = xm[:, :c1], xm[:, c1:c2], xm[:, c2:c3]
    w_lo, a_lo, g_lo = xm[:, c3:c4], xm[:, c4:c5], xm[:, c5:]
    log_w = _neg_softplus(-(w0_ref[...] + _dotf(jnp.tanh(w_lo), dup_ref[...]))) - 0.5
    log_decay = -jnp.exp(log_w)
    a = jax.nn.sigmoid(a0_ref[...] + _dotf(a_lo, aup_ref[...]))
    g = _dotf(jax.nn.sigmoid(g_lo), gup_ref[...])

    hsum = hsum_ref[...]
    kk = k * kkw_ref[...]
    kk = kk / jnp.maximum(jnp.sqrt(_dot_exact_rhs(kk * kk, hsum)), 1e-12)
    k = k * (1.0 + (a - 1.0) * kaw_ref[...])

    ti = lax.broadcasted_iota(jnp.int32, (chunk, chunk), 0)
    tj = lax.broadcasted_iota(jnp.int32, (chunk, chunk), 1)
    incl = ti >= tj
    strict = ti > tj
    cum = _dot_exact_lhs(jnp.where(incl, 1.0, 0.0).astype(BF16), log_decay)
    p_incl = jnp.exp(cum)
    p_inv = jnp.exp(-cum)
    p_end = p_incl[chunk - 1:chunk, :]
    a_t = -kk * jnp.exp(cum - log_decay)
    r_t = r * p_incl
    b_t = kk * a * p_inv
    k_t = k * p_inv
    b_e = b_t * p_end
    k_e = k_t * p_end
    eye = ti == tj

    heads = range(HEADS)
    per_head = lambda pair: [(pair[0][:, h * HEAD_DIM:(h + 1) * HEAD_DIM],
                              pair[1][:, h * HEAD_DIM:(h + 1) * HEAD_DIM]) for h in heads]
    split_all = lambda xs: [_split2(x) for x in xs]
    a_s, r_s, b_s, k_s, v_s = (per_head(_split2(x)) for x in (a_t, r_t, b_t, k_t, v))
    be_s, ke_s = per_head(_split2(b_e)), per_head(_split2(k_e))
    s0 = [state_ref[h] for h in heads]
    s0_s = split_all(s0)
    n_ab = [jnp.where(strict, _dot3(a_s[h], b_s[h], NT_DIMS), 0.0) for h in heads]
    a_ak = split_all([jnp.where(strict, _dot3(a_s[h], k_s[h], NT_DIMS), 0.0) for h in heads])
    a_rb = split_all([jnp.where(incl, _dot3(r_s[h], b_s[h], NT_DIMS), 0.0) for h in heads])
    a_rk = split_all([jnp.where(incl, _dot3(r_s[h], k_s[h], NT_DIMS), 0.0) for h in heads])
    inv = [jnp.where(eye, 1.0, 0.0) + n for n in n_ab]
    power_s = split_all(n_ab)
    for _ in range(max(int(math.log2(chunk)) - 1, 0)):
        power_s = split_all([_dot3(power_s[h], power_s[h]) for h in heads])
        inv_s = split_all(inv)
        inv = [inv[h] + _dot3(inv_s[h], power_s[h]) for h in heads]
    x_s = split_all([_dot3(a_s[h], s0_s[h]) + _dot3(a_ak[h], v_s[h]) for h in heads])
    inv_s = split_all(inv)
    u_s = split_all([_dot3(inv_s[h], x_s[h]) for h in heads])
    ys = [_dot3(r_s[h], s0_s[h]) + _dot3(a_rb[h], u_s[h]) + _dot3(a_rk[h], v_s[h]) for h in heads]
    diag = (lax.broadcasted_iota(jnp.int32, (HEAD_DIM, HEAD_DIM), 0)
            == lax.broadcasted_iota(jnp.int32, (HEAD_DIM, HEAD_DIM), 1))
    for h in heads:
        pe = p_end[:, h * HEAD_DIM:(h + 1) * HEAD_DIM]
        pe_col = jnp.sum(jnp.where(diag, pe, 0.0), axis=1, keepdims=True)
        state_ref[h] = (pe_col * s0[h] + _dot3(be_s[h], u_s[h], TN_DIMS) + _dot3(ke_s[h], v_s[h], TN_DIMS))
    y = jnp.concatenate(ys, axis=1)

    inv_n = 1.0 / HEAD_DIM
    mean = _dot_exact_rhs(y, hsum) * inv_n
    yc = y - mean
    var = _dot_exact_rhs(yc * yc, hsum) * inv_n
    yn = yc * lax.rsqrt(var + GN_EPS) * lnw_ref[...] + lnb_ref[...]
    bonus = _dot_exact_rhs(r * k * rk_ref[...], hsum) * v
    y_ref[...] = ((yn + bonus) * g).astype(y_ref.dtype)

    @pl.when(c == pl.num_programs(1) - 1)
    def _():
        for h in range(HEADS):
            sout_ref[0, h] = state_ref[h].T


def _rwkv(p_all, row0, n_seq, t_len, chunk, shift0, wkv0, rw):
    mu, w0, decay_up, a0, aaa_up, gate_up, k_k, k_a, r_k, ln_w, ln_b = rw
    n_chunks = t_len // chunk
    blk0 = row0 // chunk
    row = lambda a: a.reshape(1, -1).astype(F32)
    hsum = jnp.asarray((np.arange(WIDTH)[:, None] // HEAD_DIM == np.arange(WIDTH)[None, :] // HEAD_DIM)
                       .astype(np.float32), dtype=BF16)
    consts = [row(mu), row(w0), decay_up, row(a0), aaa_up, gate_up, row(k_k), row(k_a),
              row(r_k), row(ln_w), row(ln_b), hsum]
    const = lambda a: pl.BlockSpec(a.shape, lambda b, c: (0,) * a.ndim)
    shift3 = shift0.reshape(n_seq, 1, RW_PROJ)
    return pl.pallas_call(
        functools.partial(_rwkv_kernel, chunk=chunk),
        grid=(n_seq, n_chunks),
        in_specs=[pl.BlockSpec((chunk, RW_PROJ), lambda b, c: (blk0 + b * n_chunks + c, 0)),
                  pl.BlockSpec((1, 1, RW_PROJ), lambda b, c: (b, 0, 0)),
                  pl.BlockSpec((1, HEADS, HEAD_DIM, HEAD_DIM), lambda b, c: (b, 0, 0, 0))]
                 + [const(a) for a in consts],
        out_specs=[pl.BlockSpec((chunk, WIDTH), lambda b, c: (b * n_chunks + c, 0)),
                   pl.BlockSpec((1, HEADS, HEAD_DIM, HEAD_DIM), lambda b, c: (b, 0, 0, 0))],
        out_shape=[jax.ShapeDtypeStruct((n_seq * t_len, WIDTH), BF16),
                   jax.ShapeDtypeStruct((n_seq, HEADS, HEAD_DIM, HEAD_DIM), F32)],
        scratch_shapes=[pltpu.VMEM((HEADS, HEAD_DIM, HEAD_DIM), F32), pltpu.VMEM((1, RW_PROJ), F32)],
        compiler_params=pltpu.CompilerParams(dimension_semantics=("arbitrary", "arbitrary"),
                                             vmem_limit_bytes=VMEM_LIMIT),
        name=f"rwkv_chunk{chunk}",
    )(p_all, shift3, wkv0, *consts)


def _merge_kernel(x_ref, attn_ref, rw_ref, gate_ref, wb0_ref, wb1_ref, wo_ref, nf_ref, wr_ref, br_ref,
                  x1_ref, h2_ref, idx_ref, top_ref):
    gates = gate_ref[...]
    merged = (gates[:, :D_MODEL] * _dot(attn_ref[...], wb0_ref[...])
              + gates[:, D_MODEL:] * _dot(rw_ref[...], wb1_ref[...]))
    x1 = x_ref[...] + _dot(merged.astype(BF16), wo_ref[...])
    x1_ref[...] = x1
    h2 = _rms(x1, nf_ref[...])
    h2_ref[...] = h2
    logits = _dotf(h2, wr_ref[...]) + br_ref[...]
    lane = lax.broadcasted_iota(jnp.int32, logits.shape, 1)
    vals, idxs = [], []
    for _ in range(TOP_K):
        best = jnp.max(logits, axis=-1, keepdims=True)
        where = jnp.min(jnp.where(logits == best, lane, N_EXPERTS), axis=-1, keepdims=True)
        vals.append(best)
        idxs.append(where)
        logits = jnp.where(lane == where, -jnp.inf, logits)
    top = jnp.concatenate(vals, axis=1)
    e = jnp.exp(top - top[:, 0:1])
    top_ref[...] = e / jnp.sum(e, axis=-1, keepdims=True)
    idx_ref[...] = jnp.concatenate(idxs, axis=1)


def _merge_and_route(x, attn, rw_out, gates, w_branch, w_out, norm_ffn, w_router, b_router):
    rows = x.shape[0]
    tm = ROW_TILE
    wb0 = w_branch[0].astype(BF16)
    wb1 = w_branch[1].astype(BF16)
    wo = w_out.astype(BF16)
    nf = norm_ffn.reshape(1, D_MODEL)
    br = b_router.reshape(1, N_EXPERTS).astype(F32)
    wr = w_router.astype(F32)
    row_spec = lambda n: pl.BlockSpec((tm, n), lambda i: (i, 0))
    full = lambda a: pl.BlockSpec(a.shape, lambda i: (0,) * a.ndim)
    return pl.pallas_call(
        _merge_kernel,
        grid=(rows // tm,),
        in_specs=[row_spec(D_MODEL), row_spec(WIDTH), row_spec(WIDTH), row_spec(2 * D_MODEL),
                  full(wb0), full(wb1), full(wo), full(nf), full(wr), full(br)],
        out_specs=[row_spec(D_MODEL), row_spec(D_MODEL), row_spec(TOP_K), row_spec(TOP_K)],
        out_shape=[jax.ShapeDtypeStruct((rows, D_MODEL), F32),
                   jax.ShapeDtypeStruct((rows, D_MODEL), F32),
                   jax.ShapeDtypeStruct((rows, TOP_K), jnp.int32),
                   jax.ShapeDtypeStruct((rows, TOP_K), F32)],
        compiler_params=pltpu.CompilerParams(dimension_semantics=("arbitrary",),
                                             vmem_limit_bytes=VMEM_LIMIT),
        name="merge_route",
    )(x, attn, rw_out, gates, wb0, wb1, wo, nf, wr, br)


def _expert_kernel(be_ref, first_ref, used_ref, hs_ref, w1_ref, b1_ref, w2_ref, b2_ref, ys_ref,
                   w1b_ref, w2b_ref):
    i = pl.program_id(0)
    d_expert = w2_ref.shape[1]

    @pl.when(first_ref[i] == 1)
    def _():
        w1b_ref[...] = w1_ref[0].astype(BF16)
        w2b_ref[...] = w2_ref[0].astype(BF16)

    @pl.when(i < used_ref[0])
    def _():
        gu = _dot(hs_ref[...].astype(BF16), w1b_ref[...]) + b1_ref[0]
        gt = jnp.minimum(gu[:, :d_expert], SWIGLU_LIMIT)
        up = jnp.clip(gu[:, d_expert:], -SWIGLU_LIMIT, SWIGLU_LIMIT)
        act = (up + 1.0) * (gt * jax.nn.sigmoid(SWIGLU_ALPHA * gt))
        ys_ref[...] = _dot(act.astype(BF16), w2b_ref[...]) + b2_ref[0]

    @pl.when(i >= used_ref[0])
    def _():
        ys_ref[...] = jnp.zeros_like(ys_ref)


def _experts(hs, block_e, first, used, w1, b1, w2, b2):
    tmb = MOE_TILE
    n_blocks = hs.shape[0] // tmb
    d_expert = w2.shape[1]
    b1r = b1.reshape(N_EXPERTS, 1, 2 * d_expert)
    b2r = b2.reshape(N_EXPERTS, 1, D_MODEL)
    grid_spec = pltpu.PrefetchScalarGridSpec(
        num_scalar_prefetch=3,
        grid=(n_blocks,),
        in_specs=[pl.BlockSpec((tmb, D_MODEL), lambda i, be, fi, us: (i, 0)),
                  pl.BlockSpec((1, D_MODEL, 2 * d_expert), lambda i, be, fi, us: (be[i], 0, 0)),
                  pl.BlockSpec((1, 1, 2 * d_expert), lambda i, be, fi, us: (be[i], 0, 0)),
                  pl.BlockSpec((1, d_expert, D_MODEL), lambda i, be, fi, us: (be[i], 0, 0)),
                  pl.BlockSpec((1, 1, D_MODEL), lambda i, be, fi, us: (be[i], 0, 0))],
        out_specs=pl.BlockSpec((tmb, D_MODEL), lambda i, be, fi, us: (i, 0)),
        scratch_shapes=[pltpu.VMEM((D_MODEL, 2 * d_expert), BF16), pltpu.VMEM((d_expert, D_MODEL), BF16)],
    )
    return pl.pallas_call(
        _expert_kernel,
        grid_spec=grid_spec,
        out_shape=jax.ShapeDtypeStruct((n_blocks * tmb, D_MODEL), F32),
        compiler_params=pltpu.CompilerParams(dimension_semantics=("arbitrary",),
                                             vmem_limit_bytes=VMEM_LIMIT),
        name="experts",
    )(block_e, first, used, hs, w1, b1r, w2, b2r)


def _route_tables(top_idx, n_rows):
    tmb = MOE_TILE
    nk = n_rows * TOP_K
    n_blocks = nk // tmb + N_EXPERTS
    experts = jnp.arange(N_EXPERTS, dtype=jnp.int32)
    onehot = top_idx[:, :, None] == experts
    member = jnp.any(onehot, axis=1).astype(jnp.int32)
    counts = jnp.sum(member, axis=0)
    rank = jnp.cumsum(member, axis=0) - member
    padded = (counts + tmb - 1) // tmb * tmb
    pad_end = jnp.cumsum(padded)
    pad_start = pad_end - padded
    start = jnp.cumsum(counts) - counts
    slot_pos = jnp.sum(jnp.where(onehot, (pad_start + rank)[:, None, :], 0), axis=-1).astype(jnp.int32)
    order = jnp.argsort(top_idx.reshape(-1), stable=True).astype(jnp.int32)
    block_start = jnp.arange(n_blocks, dtype=jnp.int32) * tmb
    block_e = jnp.minimum(jnp.sum((pad_end[None, :] <= block_start[:, None]).astype(jnp.int32), axis=1),
                          N_EXPERTS - 1)
    row_e = jnp.repeat(block_e, tmb)
    in_group = jnp.arange(n_blocks * tmb, dtype=jnp.int32) - pad_start[row_e]
    valid = in_group < counts[row_e]
    sorted_slot = jnp.clip(start[row_e] + in_group, 0, nk - 1)
    row_tok = jnp.where(valid, order[sorted_slot] // TOP_K, 0).astype(jnp.int32)
    first = jnp.concatenate([jnp.ones((1,), jnp.int32),
                             (block_e[1:] != block_e[:-1]).astype(jnp.int32)])
    used = (pad_end[-1] // tmb).astype(jnp.int32).reshape(1)
    return row_tok, slot_pos, block_e.astype(jnp.int32), first, used


SC_CORES = 2
SC_SUBCORES = 16
SC_GATHER_ROWS = 64


def _gather_rows(table, idx):
    n, d = idx.shape[0], table.shape[1]
    workers = SC_CORES * SC_SUBCORES
    per_worker = n // workers
    chunk = SC_GATHER_ROWS
    assert n % workers == 0 and per_worker % chunk == 0
    mesh = plsc.VectorSubcoreMesh(core_axis_name="core", subcore_axis_name="subcore",
                                  num_cores=SC_CORES, num_subcores=SC_SUBCORES)

    @functools.partial(
        pl.kernel, mesh=mesh, out_type=jax.ShapeDtypeStruct((n, d), table.dtype),
        scratch_types=[pltpu.VMEM((chunk,), jnp.int32), pltpu.VMEM((chunk, d), table.dtype),
                       pltpu.SemaphoreType.DMA])
    def gather(table_hbm, idx_hbm, out_hbm, idx_vmem, rows_vmem, sem):
        worker = lax.axis_index("subcore") * SC_CORES + lax.axis_index("core")

        @pl.loop(0, per_worker // chunk)
        def _(i):
            base = pl.multiple_of(worker * per_worker + i * chunk, 8)
            pltpu.sync_copy(idx_hbm.at[pl.ds(base, chunk)], idx_vmem)
            pltpu.async_copy(table_hbm.at[idx_vmem], rows_vmem, sem).wait()
            pltpu.sync_copy(rows_vmem, out_hbm.at[pl.ds(base, chunk)])

    return gather(table, idx)


def _combine_kernel(x1_ref, ys_ref, gate_ref, nf_ref, o_ref):
    acc = x1_ref[...]
    gate = gate_ref[...]
    for j in range(TOP_K):
        acc = acc + gate[:, j:j + 1] * ys_ref[j]
    o_ref[...] = _rms(acc, nf_ref[...])


def _combine(x1, ys_slots, gate, norm_final):
    rows = x1.shape[0]
    tm = ROW_TILE
    nf = norm_final.reshape(1, D_MODEL)
    return pl.pallas_call(
        _combine_kernel,
        grid=(rows // tm,),
        in_specs=[pl.BlockSpec((tm, D_MODEL), lambda i: (i, 0)),
                  pl.BlockSpec((TOP_K, tm, D_MODEL), lambda i: (0, i, 0)),
                  pl.BlockSpec((tm, TOP_K), lambda i: (i, 0)),
                  pl.BlockSpec((1, D_MODEL), lambda i: (0, 0))],
        out_specs=pl.BlockSpec((tm, D_MODEL), lambda i: (i, 0)),
        out_shape=jax.ShapeDtypeStruct((rows, D_MODEL), F32),
        compiler_params=pltpu.CompilerParams(dimension_semantics=("arbitrary",),
                                             vmem_limit_bytes=VMEM_LIMIT),
        name="combine_norm",
    )(x1, ys_slots, gate, nf)


def kernel(x_prompt, x_sample, cache_k, cache_v, state_wkv, state_shift, page_table, meta, norm_mix, w_in, sb_bias, rw_mu, rw_w0, rw_decay_up, rw_a0, rw_aaa_up, rw_gate_up, rw_k_k, rw_k_a, rw_r_k, rw_ln_w, rw_ln_b, w_gate, b_gate, w_branch, w_out, norm_ffn, w_router, b_router, w1, b1, w2, b2, norm_final):
    assert w_in.shape[0] == 1, "single layer"
    batch, seq, _ = x_prompt.shape
    n_seq, n_new, _ = x_sample.shape
    t_real = N_META + seq
    t_pad = -(-t_real // ATT_TILE) * ATT_TILE
    pad = t_pad - t_real
    first_tok = pad + N_META
    assert t_pad % RWKV_CHUNK == 0 and (batch * t_pad) % ROW_TILE == 0 and (n_seq * n_new) % ROW_TILE == 0
    row_s = batch * t_pad

    meta_rows = jnp.broadcast_to(meta.astype(F32)[None], (batch, N_META, D_MODEL))
    xp = jnp.concatenate([jnp.zeros((batch, pad, D_MODEL), F32), meta_rows, x_prompt], axis=1)
    x_all = jnp.concatenate([xp.reshape(row_s, D_MODEL), x_sample.reshape(n_seq * n_new, D_MODEL)], axis=0)

    qb, k_all, v_all, kb, vb, p_all, gates = _projections(x_all, norm_mix[0], w_in[0], w_gate[0], b_gate[0])

    bias = sb_bias[0].astype(F32) * LOG2E
    attn_p = _prompt_attention(qb, kb, vb, bias, batch, t_pad, pad)
    attn_s = _paged_attention(qb, k_all, v_all, cache_k, cache_v, page_table, bias, row_s, n_seq, n_new)

    rw = (rw_mu[0], rw_w0[0], rw_decay_up[0], rw_a0[0], rw_aaa_up[0], rw_gate_up[0],
          rw_k_k[0], rw_k_a[0], rw_r_k[0], rw_ln_w[0], rw_ln_b[0])
    y_p, wkv_p = _rwkv(p_all, 0, batch, t_pad, RWKV_CHUNK,
                       jnp.zeros((batch, RW_PROJ), F32),
                       jnp.zeros((batch, HEADS, HEAD_DIM, HEAD_DIM), F32), rw)
    y_s, wkv_s = _rwkv(p_all, row_s, n_seq, n_new, n_new, state_shift[0], state_wkv[0], rw)

    attn = jnp.concatenate([attn_p, attn_s], axis=0)
    rw_out = jnp.concatenate([y_p, y_s], axis=0)
    x1, h2, top_idx, top_gate = _merge_and_route(x_all, attn, rw_out, gates, w_branch[0], w_out[0],
                                                 norm_ffn[0], w_router[0], b_router[0])

    n_rows = x_all.shape[0]
    row_tok, slot_pos, block_e, first, used = _route_tables(top_idx, n_rows)
    hs = _gather_rows(h2, row_tok)
    ys = _experts(hs, block_e, first, used, w1[0], b1[0], w2[0], b2[0])
    ys_slots = _gather_rows(ys, slot_pos.T.reshape(-1)).reshape(TOP_K, n_rows, D_MODEL)
    y_all = _combine(x1, ys_slots, top_gate, norm_final)

    def prompt_rows(a, lo):
        return a[:row_s].reshape(batch, t_pad, a.shape[-1])[:, lo:]

    y_prompt = prompt_rows(y_all, first_tok)
    y_sample = y_all[row_s:].reshape(n_seq, n_new, D_MODEL)
    k_p = prompt_rows(k_all, pad).reshape(1, batch, t_real, HEADS, HEAD_DIM)
    v_p = prompt_rows(v_all, pad).reshape(1, batch, t_real, HEADS, HEAD_DIM)
    shift_p = prompt_rows(p_all, t_pad - 1).reshape(1, batch, RW_PROJ)
    k_s = k_all[row_s:].reshape(1, n_seq, n_new, HEADS, HEAD_DIM)
    v_s = v_all[row_s:].reshape(1, n_seq, n_new, HEADS, HEAD_DIM)
    shift_s = p_all[row_s:].reshape(n_seq, n_new, RW_PROJ)[:, -1].reshape(1, n_seq, RW_PROJ)
    return (y_prompt, y_sample, k_p, v_p, wkv_p[None], shift_p, k_s, v_s, wkv_s[None], shift_s)
```

```python
import functools
import math

import numpy as np
import jax
import jax.numpy as jnp
from jax import lax
from jax.experimental import pallas as pl
from jax.experimental.pallas import tpu as pltpu
from jax.experimental.pallas import tpu_sc as plsc

F32 = jnp.float32
BF16 = jnp.bfloat16
HIGHEST = lax.Precision.HIGHEST

D_MODEL = 1024
N_META = 16
HEADS = 8
HEAD_DIM = 64
WIDTH = HEADS * HEAD_DIM
LOG2E = math.log2(math.e)
SB_SCALE2 = LOG2E / math.sqrt(HEAD_DIM)
DECAY_LORA = 64
AAA_LORA = 64
GATE_LORA = 128
RW_PROJ = 3 * WIDTH + DECAY_LORA + AAA_LORA + GATE_LORA
GN_EPS = 64e-5
N_EXPERTS = 32
TOP_K = 4
SWIGLU_LIMIT = 7.0
SWIGLU_ALPHA = 1.702
RMS_EPS = 1e-5

ROW_TILE = 256
ATT_TILE = 256
RWKV_CHUNK = 64
PAGES_PER_STEP = 8
MOE_TILE = 256
VMEM_LIMIT = 56 * 1024 * 1024

NT_DIMS = (((1,), (1,)), ((), ()))
TN_DIMS = (((0,), (0,)), ((), ()))
NN_DIMS = (((1,), (0,)), ((), ()))


def _dot(a, b):
    return jnp.dot(a, b, preferred_element_type=F32)


def _dotf(a, b, dims=NN_DIMS):
    return lax.dot_general(a, b, dims, precision=HIGHEST, preferred_element_type=F32)


def _split2(x):
    hi = x.astype(BF16)
    return hi, (x - hi.astype(F32)).astype(BF16)


def _dot3(a, b, dims=NN_DIMS):
    dg = lambda x, y: lax.dot_general(x, y, dims, preferred_element_type=F32)
    return dg(a[0], b[0]) + dg(a[0], b[1]) + dg(a[1], b[0])


def _split3(x):
    hi = x.astype(BF16)
    r1 = x - hi.astype(F32)
    mid = r1.astype(BF16)
    lo = (r1 - mid.astype(F32)).astype(BF16)
    return hi, mid, lo


def _dot_exact_rhs(x, m):
    hi, mid, lo = _split3(x)
    return _dot(hi, m) + _dot(mid, m) + _dot(lo, m)


def _dot_exact_lhs(m, x):
    hi, mid, lo = _split3(x)
    return _dot(m, hi) + _dot(m, mid) + _dot(m, lo)


def _neg_softplus(z):
    return jnp.minimum(-z, 0.0) - jnp.log1p(jnp.exp(-jnp.abs(z)))


def _rms(x, g):
    return x * lax.rsqrt(jnp.mean(x * x, axis=-1, keepdims=True) + RMS_EPS) * g


def _proj_kernel(x_ref, g_ref, wqkv_ref, wp_ref, wg_ref, bg_ref,
                 q_ref, k_ref, v_ref, kb_ref, vb_ref, p_ref, gate_ref):
    hb = _rms(x_ref[...], g_ref[...]).astype(BF16)
    qkv = _dot(hb, wqkv_ref[...])
    k = qkv[:, WIDTH:2 * WIDTH]
    v = qkv[:, 2 * WIDTH:]
    q_ref[...] = (qkv[:, :WIDTH] * SB_SCALE2).astype(BF16)
    k_ref[...] = k
    v_ref[...] = v
    kb_ref[...] = k.astype(BF16)
    vb_ref[...] = v.astype(BF16)
    p_ref[...] = _dot(hb, wp_ref[...])
    gate_ref[...] = jax.nn.sigmoid(_dot(hb, wg_ref[...]) + bg_ref[...])


def _projections(x, norm_g, w_in, w_gate, b_gate):
    rows = x.shape[0]
    tm = ROW_TILE
    wqkv = w_in[:, :3 * WIDTH].astype(BF16)
    wp = w_in[:, 3 * WIDTH:].astype(BF16)
    wg = w_gate.astype(BF16)
    row_spec = lambda n: pl.BlockSpec((tm, n), lambda i: (i, 0))
    full = lambda a: pl.BlockSpec(a.shape, lambda i: (0,) * a.ndim)
    g2 = norm_g.reshape(1, D_MODEL)
    bg = b_gate.reshape(1, 2 * D_MODEL)
    return pl.pallas_call(
        _proj_kernel,
        grid=(rows // tm,),
        in_specs=[row_spec(D_MODEL), full(g2), full(wqkv), full(wp), full(wg), full(bg)],
        out_specs=[row_spec(WIDTH)] * 5 + [row_spec(RW_PROJ), row_spec(2 * D_MODEL)],
        out_shape=[jax.ShapeDtypeStruct((rows, WIDTH), BF16),
                   jax.ShapeDtypeStruct((rows, WIDTH), F32),
                   jax.ShapeDtypeStruct((rows, WIDTH), F32),
                   jax.ShapeDtypeStruct((rows, WIDTH), BF16),
                   jax.ShapeDtypeStruct((rows, WIDTH), BF16),
                   jax.ShapeDtypeStruct((rows, RW_PROJ), F32),
                   jax.ShapeDtypeStruct((rows, 2 * D_MODEL), F32)],
        compiler_params=pltpu.CompilerParams(dimension_semantics=("arbitrary",),
                                             vmem_limit_bytes=VMEM_LIMIT),
        name="projections",
    )(x, g2, wqkv, wp, wg, bg)


def _stick_logs(z, visible):
    neg_abs = lax.bitcast_convert_type(
        lax.bitcast_convert_type(z, jnp.uint32) | jnp.uint32(0x80000000), F32)
    sp = jnp.maximum(z, 0.0) + jnp.log2(1.0 + jnp.exp2(neg_abs))
    log_beta = z - sp
    if visible is not None:
        sp = jnp.where(visible, sp, 0.0)
    hi = sp.astype(BF16)
    lo = (sp - hi.astype(F32)).astype(BF16)
    return jnp.concatenate([hi, lo], axis=1), log_beta, jnp.sum(sp, axis=-1, keepdims=True)


def _stick_weights(log_beta, after, visible):
    w = jnp.exp2(log_beta - after)
    if visible is not None:
        w = jnp.where(visible, w, 0.0)
    return w


def _suffix_matrix(n):
    j = np.arange(n)
    m = (j[:, None] > j[None, :]).astype(np.float32)
    return jnp.asarray(np.concatenate([m, m], axis=0), dtype=BF16)


def _prompt_attn_kernel(qi_ref, kt_ref, bias_ref, q_ref, k1_ref, v1_ref, k0_ref, v0_ref, m_ref, o_ref,
                        acc_ref, carry_ref, *, tile, pad):
    t = pl.program_id(1)
    i = qi_ref[t]
    j0 = kt_ref[t]
    first = j0 + 1 >= i

    @pl.when(first)
    def _():
        acc_ref[...] = jnp.zeros_like(acc_ref)
        carry_ref[...] = jnp.zeros_like(carry_ref)

    t_far = pl.num_programs(1) * 2 * tile
    subs = ((j0 + 1, k1_ref, v1_ref), (j0, k0_ref, v0_ref))

    def process(masked):
        q, m2 = q_ref[...], m_ref[...]
        head = lambda h: slice(h * HEAD_DIM, (h + 1) * HEAD_DIM)
        items = []
        for j, k_ref, v_ref in subs:
            visible = None
            if masked:
                q_pos = i * tile + lax.broadcasted_iota(jnp.int32, (tile, tile), 0)
                k_pos = j * tile + lax.broadcasted_iota(jnp.int32, (tile, tile), 1)
                visible = jnp.where(k_pos >= pad, k_pos, t_far) < q_pos
            k, v = k_ref[...], v_ref[...]
            items += [(h, k, v, visible) for h in range(HEADS)]
        scores, logs, afters = {}, {}, {}
        outs = [None] * HEADS
        for r in range(len(items) + 2):
            if r < len(items):
                h, k, _, _ = items[r]
                scores[r] = lax.dot_general(q[:, head(h)], k[:, head(h)], NT_DIMS,
                                            preferred_element_type=F32)
            if 0 <= r - 1 < len(items):
                h, _, _, visible = items[r - 1]
                hilo, log_beta, total = _stick_logs(scores.pop(r - 1) + bias_ref[h], visible)
                afters[r - 1] = _dot(hilo, m2)
                logs[r - 1] = (log_beta, total)
            if 0 <= r - 2 < len(items):
                h, _, v, visible = items[r - 2]
                log_beta, total = logs.pop(r - 2)
                w = _stick_weights(log_beta, afters.pop(r - 2), visible)
                carry = carry_ref[h]
                out = jnp.exp2(-carry) * _dot(w.astype(BF16), v[:, head(h)])
                outs[h] = out if outs[h] is None else outs[h] + out
                carry_ref[h] = carry + total
        acc_ref[...] += jnp.concatenate(outs, axis=1)

    needs_mask = jnp.logical_or(first, j0 == 0)
    pl.when(needs_mask)(lambda: process(True))
    pl.when(jnp.logical_not(needs_mask))(lambda: process(False))

    @pl.when(j0 == 0)
    def _():
        o_ref[...] = acc_ref[...].astype(o_ref.dtype)


def _prompt_attention(qb, kb, vb, bias2, batch, t_pad, pad):
    tile = ATT_TILE
    nq = t_pad // tile
    qi = np.concatenate([np.full(i // 2 + 1, i) for i in range(nq)]).astype(np.int32)
    kt = np.concatenate([2 * np.arange(i // 2, -1, -1) for i in range(nq)]).astype(np.int32)
    q_map = lambda b, t, qi, kt: (b * nq + qi[t], 0)
    hi_map = lambda b, t, qi, kt: (b * nq + jnp.minimum(kt[t] + 1, nq - 1), 0)
    lo_map = lambda b, t, qi, kt: (b * nq + kt[t], 0)
    grid_spec = pltpu.PrefetchScalarGridSpec(
        num_scalar_prefetch=2,
        grid=(batch, len(qi)),
        in_specs=[
            pl.BlockSpec(memory_space=pltpu.SMEM),
            pl.BlockSpec((tile, WIDTH), q_map),
            pl.BlockSpec((tile, WIDTH), hi_map),
            pl.BlockSpec((tile, WIDTH), hi_map),
            pl.BlockSpec((tile, WIDTH), lo_map),
            pl.BlockSpec((tile, WIDTH), lo_map),
            pl.BlockSpec((2 * tile, tile), lambda b, t, qi, kt: (0, 0)),
        ],
        out_specs=pl.BlockSpec((tile, WIDTH), q_map),
        scratch_shapes=[pltpu.VMEM((tile, WIDTH), F32), pltpu.VMEM((HEADS, tile, 1), F32)],
    )
    return pl.pallas_call(
        functools.partial(_prompt_attn_kernel, tile=tile, pad=pad),
        grid_spec=grid_spec,
        out_shape=jax.ShapeDtypeStruct((batch * t_pad, WIDTH), BF16),
        compiler_params=pltpu.CompilerParams(dimension_semantics=("arbitrary", "arbitrary"),
                                             vmem_limit_bytes=VMEM_LIMIT),
        name="prompt_attention",
    )(jnp.asarray(qi), jnp.asarray(kt), bias2, qb, kb, vb, kb, vb, _suffix_matrix(tile))


def _paged_attn_kernel(pt_ref, bias_ref, hmask_ref, m_ref, q_ref, kn_ref, vn_ref, *rest,
                       n_new, page, pages_per_step):
    page_refs = rest[:2 * pages_per_step]
    o_ref = rest[2 * pages_per_step]
    qbd_ref, acc_ref, carry_ref, knew_ref, vnew_ref = rest[2 * pages_per_step + 1:]
    s = pl.program_id(1)
    rows = HEADS * n_new

    def scores(keys, dim_major):
        dims = NN_DIMS if dim_major else NT_DIMS
        return (lax.dot_general(qbd_ref[...], keys.astype(BF16), dims, preferred_element_type=F32)
                + bias_ref[...])

    def logs(z, visible):
        hilo, log_beta, total = _stick_logs(z, visible)
        return _dot(hilo, m_ref[...]), log_beta, total

    def accumulate(after, log_beta, total, values, dim_major, visible):
        w = _stick_weights(log_beta, after, visible).astype(BF16)
        dims = NT_DIMS if dim_major else NN_DIMS
        out = lax.dot_general(w, values.astype(BF16), dims, preferred_element_type=F32)
        carry = carry_ref[...]
        acc_ref[...] += jnp.exp2(-carry) * out
        carry_ref[...] = carry + total

    @pl.when(s == 0)
    def _():
        q_rep = jnp.concatenate([q_ref[...]] * HEADS, axis=0)
        qbd_ref[...] = jnp.where(hmask_ref[...] > 0.0, q_rep, jnp.zeros_like(q_rep))
        acc_ref[...] = jnp.zeros_like(acc_ref)
        carry_ref[...] = jnp.zeros_like(carry_ref)
        knew_ref[...] = jnp.zeros_like(knew_ref)
        vnew_ref[...] = jnp.zeros_like(vnew_ref)
        knew_ref[0:n_new, :] = kn_ref[...]
        vnew_ref[0:n_new, :] = vn_ref[...]
        t_idx = lax.rem(lax.broadcasted_iota(jnp.int32, (rows, page), 0), n_new)
        c_idx = lax.broadcasted_iota(jnp.int32, (rows, page), 1)
        visible = c_idx < t_idx
        after, log_beta, total = logs(scores(knew_ref[...], False), visible)
        accumulate(after, log_beta, total, vnew_ref[...], False, visible)

    def channels(ref):
        return ref[0, 0].reshape(WIDTH, page)

    n = pages_per_step
    z, lg = {}, {}
    for r in range(n + 2):
        if r < n:
            z[r] = scores(channels(page_refs[2 * r]), True)
        if 0 <= r - 1 < n:
            lg[r - 1] = logs(z.pop(r - 1), None)
        if 0 <= r - 2 < n:
            accumulate(*lg.pop(r - 2), channels(page_refs[2 * (r - 2) + 1]), True, None)

    @pl.when(s == pl.num_programs(1) - 1)
    def _():
        own = acc_ref[...] * hmask_ref[...]
        o_ref[...] = own.reshape(HEADS, n_new, WIDTH).sum(axis=0).astype(o_ref.dtype)


def _paged_attention(qb, k_new, v_new, k_pool, v_pool, page_table, bias, row0, n_seq, n_new):
    n_pages = page_table.shape[1]
    page = k_pool.shape[2]
    k_pool = jnp.transpose(k_pool, (0, 1, 3, 4, 2))
    v_pool = jnp.transpose(v_pool, (0, 1, 3, 4, 2))
    pps = PAGES_PER_STEP
    steps = n_pages // pps
    rows = HEADS * n_new
    blk0 = row0 // n_new
    bias_col = jnp.repeat(bias.astype(F32), n_new).reshape(rows, 1)
    hmask = jnp.asarray((np.arange(rows)[:, None] // n_new == np.arange(WIDTH)[None, :] // HEAD_DIM)
                        .astype(np.float32))
    const = lambda a: pl.BlockSpec(a.shape, lambda b, s, pt: (0,) * a.ndim)
    new_spec = pl.BlockSpec((n_new, WIDTH), lambda b, s, pt: (blk0 + b, 0))

    def page_spec(p):
        return pl.BlockSpec((1, 1, HEADS, HEAD_DIM, page),
                            lambda b, s, pt: (0, pt[b, n_pages - 1 - (s * pps + p)], 0, 0, 0))

    page_specs, page_args = [], []
    for p in range(pps):
        page_specs += [page_spec(p), page_spec(p)]
        page_args += [k_pool, v_pool]
    m = _suffix_matrix(page)
    grid_spec = pltpu.PrefetchScalarGridSpec(
        num_scalar_prefetch=1,
        grid=(n_seq, steps),
        in_specs=[const(bias_col), const(hmask), const(m), new_spec, new_spec, new_spec] + page_specs,
        out_specs=pl.BlockSpec((n_new, WIDTH), lambda b, s, pt: (b, 0)),
        scratch_shapes=[pltpu.VMEM((rows, WIDTH), BF16), pltpu.VMEM((rows, WIDTH), F32),
                        pltpu.VMEM((rows, 1), F32), pltpu.VMEM((page, WIDTH), F32),
                        pltpu.VMEM((page, WIDTH), F32)],
    )
    return pl.pallas_call(
        functools.partial(_paged_attn_kernel, n_new=n_new, page=page, pages_per_step=pps),
        grid_spec=grid_spec,
        out_shape=jax.ShapeDtypeStruct((n_seq * n_new, WIDTH), BF16),
        compiler_params=pltpu.CompilerParams(dimension_semantics=("arbitrary", "arbitrary"),
                                             vmem_limit_bytes=VMEM_LIMIT),
        name="paged_attention",
    )(page_table, bias_col, hmask, m, qb, k_new, v_new, *page_args)


def _rwkv_kernel(p_ref, shift_ref, s0_ref, mu_ref, w0_ref, dup_ref, a0_ref, aup_ref, gup_ref,
                 kkw_ref, kaw_ref, rk_ref, lnw_ref, lnb_ref, hsum_ref,
                 y_ref, sout_ref, state_ref, prev_ref, *, chunk):
    c = pl.program_id(1)

    @pl.when(c == 0)
    def _():
        for h in range(HEADS):
            state_ref[h] = s0_ref[0, h].T
        prev_ref[...] = shift_ref[0]

    p = p_ref[...]
    row = lax.broadcasted_iota(jnp.int32, p.shape, 0)
    prev = jnp.where(row == 0, prev_ref[...], pltpu.roll(p, 1, 0))
    prev_ref[...] = p[chunk - 1:chunk, :]
    xm = p + (prev - p) * mu_ref[...]

    c1, c2, c3 = WIDTH, 2 * WIDTH, 3 * WIDTH
    c4 = c3 + DECAY_LORA
    c5 = c4 + AAA_LORA
    r, k, v = xm[:, :c1], xm[:, c1:c2], xm[:, c2:c3]
    w_lo, a_lo, g_lo = xm[:, c3:c4], xm[:, c4:c5], xm[:, c5:]
    log_w = _neg_softplus(-(w0_ref[...] + _dotf(jnp.tanh(w_lo), dup_ref[...]))) - 0.5
    log_decay = -jnp.exp(log_w)
    a = jax.nn.sigmoid(a0_ref[...] + _dotf(a_lo, aup_ref[...]))
    g = _dotf(jax.nn.sigmoid(g_lo), gup_ref[...])

    hsum = hsum_ref[...]
    kk = k * kkw_ref[...]
    kk = kk / jnp.maximum(jnp.sqrt(_dot_exact_rhs(kk * kk, hsum)), 1e-12)
    k = k * (1.0 + (a - 1.0) * kaw_ref[...])

    ti = lax.broadcasted_iota(jnp.int32, (chunk, chunk), 0)
    tj = lax.broadcasted_iota(jnp.int32, (chunk, chunk), 1)
    incl = ti >= tj
    strict = ti > tj
    cum = _dot_exact_lhs(jnp.where(incl, 1.0, 0.0).astype(BF16), log_decay)
    p_incl = jnp.exp(cum)
    p_inv = jnp.exp(-cum)
    p_end = p_incl[chunk - 1:chunk, :]
    a_t = -kk * jnp.exp(cum - log_decay)
    r_t = r * p_incl
    b_t = kk * a * p_inv
    k_t = k * p_inv
    b_e = b_t * p_end
    k_e = k_t * p_end
    eye = ti == tj

    heads = range(HEADS)
    per_head = lambda pair: [(pair[0][:, h * HEAD_DIM:(h + 1) * HEAD_DIM],
                              pair[1][:, h * HEAD_DIM:(h + 1) * HEAD_DIM]) for h in heads]
    split_all = lambda xs: [_split2(x) for x in xs]
    a_s, r_s, b_s, k_s, v_s = (per_head(_split2(x)) for x in (a_t, r_t, b_t, k_t, v))
    be_s, ke_s = per_head(_split2(b_e)), per_head(_split2(k_e))
    s0 = [state_ref[h] for h in heads]
    s0_s = split_all(s0)
    stack = lambda p, q: (jnp.concatenate([p[0], q[0]], axis=0), jnp.concatenate([p[1], q[1]], axis=0))
    stacked = chunk % 16 == 0
    if stacked:
        ar_s = [stack(a_s[h], r_s[h]) for h in heads]
        quad = [_dot3(ar_s[h], stack(b_s[h], k_s[h]), NT_DIMS) for h in heads]
        n_ab = [jnp.where(strict, quad[h][:chunk, :chunk], 0.0) for h in heads]
        a_ak = split_all([jnp.where(strict, quad[h][:chunk, chunk:], 0.0) for h in heads])
        col2 = lax.broadcasted_iota(jnp.int32, (chunk, 2 * chunk), 1)
        incl2 = (lax.broadcasted_iota(jnp.int32, (chunk, 2 * chunk), 0)
                 >= jnp.where(col2 >= chunk, col2 - chunk, col2))
        a_rbk = split_all([jnp.where(incl2, quad[h][chunk:, :], 0.0) for h in heads])
        on_state = [_dot3(ar_s[h], s0_s[h]) for h in heads]
        x_state = [t[:chunk] for t in on_state]
        y_state = [t[chunk:] for t in on_state]
    else:
        n_ab = [jnp.where(strict, _dot3(a_s[h], b_s[h], NT_DIMS), 0.0) for h in heads]
        a_ak = split_all([jnp.where(strict, _dot3(a_s[h], k_s[h], NT_DIMS), 0.0) for h in heads])
        a_rb = split_all([jnp.where(incl, _dot3(r_s[h], b_s[h], NT_DIMS), 0.0) for h in heads])
        a_rk = split_all([jnp.where(incl, _dot3(r_s[h], k_s[h], NT_DIMS), 0.0) for h in heads])
        x_state = [_dot3(a_s[h], s0_s[h]) for h in heads]
        y_state = [_dot3(r_s[h], s0_s[h]) for h in heads]
    inv = [jnp.where(eye, 1.0, 0.0) + n for n in n_ab]
    power_s = split_all(n_ab)
    for _ in range(max(int(math.log2(chunk)) - 1, 0)):
        power_s = split_all([_dot3(power_s[h], power_s[h]) for h in heads])
        inv_s = split_all(inv)
        inv = [inv[h] + _dot3(inv_s[h], power_s[h]) for h in heads]
    x_s = split_all([x_state[h] + _dot3(a_ak[h], v_s[h]) for h in heads])
    inv_s = split_all(inv)
    u_s = split_all([_dot3(inv_s[h], x_s[h]) for h in heads])
    diag = (lax.broadcasted_iota(jnp.int32, (HEAD_DIM, HEAD_DIM), 0)
            == lax.broadcasted_iota(jnp.int32, (HEAD_DIM, HEAD_DIM), 1))
    if stacked:
        uv_s = [stack(u_s[h], v_s[h]) for h in heads]
        ys = [y_state[h] + _dot3(a_rbk[h], uv_s[h]) for h in heads]
        grown = [_dot3(stack(be_s[h], ke_s[h]), uv_s[h], TN_DIMS) for h in heads]
    else:
        ys = [y_state[h] + _dot3(a_rb[h], u_s[h]) + _dot3(a_rk[h], v_s[h]) for h in heads]
        grown = [_dot3(be_s[h], u_s[h], TN_DIMS) + _dot3(ke_s[h], v_s[h], TN_DIMS) for h in heads]
    for h in heads:
        pe = p_end[:, h * HEAD_DIM:(h + 1) * HEAD_DIM]
        pe_col = jnp.sum(jnp.where(diag, pe, 0.0), axis=1, keepdims=True)
        state_ref[h] = pe_col * s0[h] + grown[h]
    y = jnp.concatenate(ys, axis=1)

    inv_n = 1.0 / HEAD_DIM
    mean = _dot_exact_rhs(y, hsum) * inv_n
    yc = y - mean
    var = _dot_exact_rhs(yc * yc, hsum) * inv_n
    yn = yc * lax.rsqrt(var + GN_EPS) * lnw_ref[...] + lnb_ref[...]
    bonus = _dot_exact_rhs(r * k * rk_ref[...], hsum) * v
    y_ref[...] = ((yn + bonus) * g).astype(y_ref.dtype)

    @pl.when(c == pl.num_programs(1) - 1)
    def _():
        for h in range(HEADS):
            sout_ref[0, h] = state_ref[h].T


def _rwkv(p_all, row0, n_seq, t_len, chunk, shift0, wkv0, rw):
    mu, w0, decay_up, a0, aaa_up, gate_up, k_k, k_a, r_k, ln_w, ln_b = rw
    n_chunks = t_len // chunk
    blk0 = row0 // chunk
    row = lambda a: a.reshape(1, -1).astype(F32)
    hsum = jnp.asarray((np.arange(WIDTH)[:, None] // HEAD_DIM == np.arange(WIDTH)[None, :] // HEAD_DIM)
                       .astype(np.float32), dtype=BF16)
    consts = [row(mu), row(w0), decay_up, row(a0), aaa_up, gate_up, row(k_k), row(k_a),
              row(r_k), row(ln_w), row(ln_b), hsum]
    const = lambda a: pl.BlockSpec(a.shape, lambda b, c: (0,) * a.ndim)
    shift3 = shift0.reshape(n_seq, 1, RW_PROJ)
    return pl.pallas_call(
        functools.partial(_rwkv_kernel, chunk=chunk),
        grid=(n_seq, n_chunks),
        in_specs=[pl.BlockSpec((chunk, RW_PROJ), lambda b, c: (blk0 + b * n_chunks + c, 0)),
                  pl.BlockSpec((1, 1, RW_PROJ), lambda b, c: (b, 0, 0)),
                  pl.BlockSpec((1, HEADS, HEAD_DIM, HEAD_DIM), lambda b, c: (b, 0, 0, 0))]
                 + [const(a) for a in consts],
        out_specs=[pl.BlockSpec((chunk, WIDTH), lambda b, c: (b * n_chunks + c, 0)),
                   pl.BlockSpec((1, HEADS, HEAD_DIM, HEAD_DIM), lambda b, c: (b, 0, 0, 0))],
        out_shape=[jax.ShapeDtypeStruct((n_seq * t_len, WIDTH), BF16),
                   jax.ShapeDtypeStruct((n_seq, HEADS, HEAD_DIM, HEAD_DIM), F32)],
        scratch_shapes=[pltpu.VMEM((HEADS, HEAD_DIM, HEAD_DIM), F32), pltpu.VMEM((1, RW_PROJ), F32)],
        compiler_params=pltpu.CompilerParams(dimension_semantics=("arbitrary", "arbitrary"),
                                             vmem_limit_bytes=VMEM_LIMIT),
        name=f"rwkv_chunk{chunk}",
    )(p_all, shift3, wkv0, *consts)


def _merge_kernel(x_ref, attn_ref, rw_ref, gate_ref, wb0_ref, wb1_ref, wo_ref, nf_ref, wr_ref, br_ref,
                  x1_ref, h2_ref, idx_ref, top_ref):
    gates = gate_ref[...]
    merged = (gates[:, :D_MODEL] * _dot(attn_ref[...], wb0_ref[...])
              + gates[:, D_MODEL:] * _dot(rw_ref[...], wb1_ref[...]))
    x1 = x_ref[...] + _dot(merged.astype(BF16), wo_ref[...])
    x1_ref[...] = x1
    h2 = _rms(x1, nf_ref[...])
    h2_ref[...] = h2
    logits = _dotf(h2, wr_ref[...]) + br_ref[...]
    lane = lax.broadcasted_iota(jnp.int32, logits.shape, 1)
    vals, idxs = [], []
    for _ in range(TOP_K):
        best = jnp.max(logits, axis=-1, keepdims=True)
        where = jnp.min(jnp.where(logits == best, lane, N_EXPERTS), axis=-1, keepdims=True)
        vals.append(best)
        idxs.append(where)
        logits = jnp.where(lane == where, -jnp.inf, logits)
    top = jnp.concatenate(vals, axis=1)
    e = jnp.exp(top - top[:, 0:1])
    top_ref[...] = e / jnp.sum(e, axis=-1, keepdims=True)
    idx_ref[...] = jnp.concatenate(idxs, axis=1)


def _merge_and_route(x, attn, rw_out, gates, w_branch, w_out, norm_ffn, w_router, b_router):
    rows = x.shape[0]
    tm = ROW_TILE
    wb0 = w_branch[0].astype(BF16)
    wb1 = w_branch[1].astype(BF16)
    wo = w_out.astype(BF16)
    nf = norm_ffn.reshape(1, D_MODEL)
    br = b_router.reshape(1, N_EXPERTS).astype(F32)
    wr = w_router.astype(F32)
    row_spec = lambda n: pl.BlockSpec((tm, n), lambda i: (i, 0))
    full = lambda a: pl.BlockSpec(a.shape, lambda i: (0,) * a.ndim)
    return pl.pallas_call(
        _merge_kernel,
        grid=(rows // tm,),
        in_specs=[row_spec(D_MODEL), row_spec(WIDTH), row_spec(WIDTH), row_spec(2 * D_MODEL),
                  full(wb0), full(wb1), full(wo), full(nf), full(wr), full(br)],
        out_specs=[row_spec(D_MODEL), row_spec(D_MODEL), row_spec(TOP_K), row_spec(TOP_K)],
        out_shape=[jax.ShapeDtypeStruct((rows, D_MODEL), F32),
                   jax.ShapeDtypeStruct((rows, D_MODEL), F32),
                   jax.ShapeDtypeStruct((rows, TOP_K), jnp.int32),
                   jax.ShapeDtypeStruct((rows, TOP_K), F32)],
        compiler_params=pltpu.CompilerParams(dimension_semantics=("arbitrary",),
                                             vmem_limit_bytes=VMEM_LIMIT),
        name="merge_route",
    )(x, attn, rw_out, gates, wb0, wb1, wo, nf, wr, br)


def _expert_kernel(be_ref, first_ref, used_ref, hs_ref, w1_ref, b1_ref, w2_ref, b2_ref, ys_ref,
                   w1b_ref, w2b_ref):
    i = pl.program_id(0)
    d_expert = w2_ref.shape[1]

    @pl.when(first_ref[i] == 1)
    def _():
        w1b_ref[...] = w1_ref[0].astype(BF16)
        w2b_ref[...] = w2_ref[0].astype(BF16)

    @pl.when(i < used_ref[0])
    def _():
        gu = _dot(hs_ref[...].astype(BF16), w1b_ref[...]) + b1_ref[0]
        gt = jnp.minimum(gu[:, :d_expert], SWIGLU_LIMIT)
        up = jnp.clip(gu[:, d_expert:], -SWIGLU_LIMIT, SWIGLU_LIMIT)
        act = (up + 1.0) * (gt * jax.nn.sigmoid(SWIGLU_ALPHA * gt))
        ys_ref[...] = _dot(act.astype(BF16), w2b_ref[...]) + b2_ref[0]

    @pl.when(i >= used_ref[0])
    def _():
        ys_ref[...] = jnp.zeros_like(ys_ref)


def _experts(hs, block_e, first, used, w1, b1, w2, b2):
    tmb = MOE_TILE
    n_blocks = hs.shape[0] // tmb
    d_expert = w2.shape[1]
    b1r = b1.reshape(N_EXPERTS, 1, 2 * d_expert)
    b2r = b2.reshape(N_EXPERTS, 1, D_MODEL)
    grid_spec = pltpu.PrefetchScalarGridSpec(
        num_scalar_prefetch=3,
        grid=(n_blocks,),
        in_specs=[pl.BlockSpec((tmb, D_MODEL), lambda i, be, fi, us: (i, 0)),
                  pl.BlockSpec((1, D_MODEL, 2 * d_expert), lambda i, be, fi, us: (be[i], 0, 0)),
                  pl.BlockSpec((1, 1, 2 * d_expert), lambda i, be, fi, us: (be[i], 0, 0)),
                  pl.BlockSpec((1, d_expert, D_MODEL), lambda i, be, fi, us: (be[i], 0, 0)),
                  pl.BlockSpec((1, 1, D_MODEL), lambda i, be, fi, us: (be[i], 0, 0))],
        out_specs=pl.BlockSpec((tmb, D_MODEL), lambda i, be, fi, us: (i, 0)),
        scratch_shapes=[pltpu.VMEM((D_MODEL, 2 * d_expert), BF16), pltpu.VMEM((d_expert, D_MODEL), BF16)],
    )
    return pl.pallas_call(
        _expert_kernel,
        grid_spec=grid_spec,
        out_shape=jax.ShapeDtypeStruct((n_blocks * tmb, D_MODEL), F32),
        compiler_params=pltpu.CompilerParams(dimension_semantics=("arbitrary",),
                                             vmem_limit_bytes=VMEM_LIMIT),
        name="experts",
    )(block_e, first, used, hs, w1, b1r, w2, b2r)


def _route_tables(top_idx, n_rows):
    tmb = MOE_TILE
    nk = n_rows * TOP_K
    n_blocks = nk // tmb + N_EXPERTS
    experts = jnp.arange(N_EXPERTS, dtype=jnp.int32)
    onehot = top_idx[:, :, None] == experts
    member = jnp.any(onehot, axis=1).astype(jnp.int32)
    counts = jnp.sum(member, axis=0)
    rank = jnp.cumsum(member, axis=0) - member
    padded = (counts + tmb - 1) // tmb * tmb
    pad_end = jnp.cumsum(padded)
    pad_start = pad_end - padded
    start = jnp.cumsum(counts) - counts
    slot_pos = jnp.sum(jnp.where(onehot, (pad_start + rank)[:, None, :], 0), axis=-1).astype(jnp.int32)
    order = jnp.argsort(top_idx.reshape(-1), stable=True).astype(jnp.int32)
    block_start = jnp.arange(n_blocks, dtype=jnp.int32) * tmb
    block_e = jnp.minimum(jnp.sum((pad_end[None, :] <= block_start[:, None]).astype(jnp.int32), axis=1),
                          N_EXPERTS - 1)
    row_e = jnp.repeat(block_e, tmb)
    in_group = jnp.arange(n_blocks * tmb, dtype=jnp.int32) - pad_start[row_e]
    valid = in_group < counts[row_e]
    sorted_slot = jnp.clip(start[row_e] + in_group, 0, nk - 1)
    row_tok = jnp.where(valid, order[sorted_slot] // TOP_K, 0).astype(jnp.int32)
    first = jnp.concatenate([jnp.ones((1,), jnp.int32),
                             (block_e[1:] != block_e[:-1]).astype(jnp.int32)])
    used = (pad_end[-1] // tmb).astype(jnp.int32).reshape(1)
    return row_tok, slot_pos, block_e.astype(jnp.int32), first, used


SC_CORES = 2
SC_SUBCORES = 16
SC_GATHER_ROWS = 64


def _gather_rows(table, idx):
    n, d = idx.shape[0], table.shape[1]
    workers = SC_CORES * SC_SUBCORES
    per_worker = n // workers
    chunk = SC_GATHER_ROWS
    assert n % workers == 0 and per_worker % chunk == 0
    mesh = plsc.VectorSubcoreMesh(core_axis_name="core", subcore_axis_name="subcore",
                                  num_cores=SC_CORES, num_subcores=SC_SUBCORES)

    @functools.partial(
        pl.kernel, mesh=mesh, out_type=jax.ShapeDtypeStruct((n, d), table.dtype),
        scratch_types=[pltpu.VMEM((chunk,), jnp.int32), pltpu.VMEM((chunk, d), table.dtype),
                       pltpu.SemaphoreType.DMA])
    def gather(table_hbm, idx_hbm, out_hbm, idx_vmem, rows_vmem, sem):
        worker = lax.axis_index("subcore") * SC_CORES + lax.axis_index("core")

        @pl.loop(0, per_worker // chunk)
        def _(i):
            base = pl.multiple_of(worker * per_worker + i * chunk, 8)
            pltpu.sync_copy(idx_hbm.at[pl.ds(base, chunk)], idx_vmem)
            pltpu.async_copy(table_hbm.at[idx_vmem], rows_vmem, sem).wait()
            pltpu.sync_copy(rows_vmem, out_hbm.at[pl.ds(base, chunk)])

    return gather(table, idx)


def _combine_kernel(x1_ref, ys_ref, gate_ref, nf_ref, yp_ref, ys_out_ref, *, prompt_tiles):
    acc = x1_ref[...]
    gate = gate_ref[...]
    for j in range(TOP_K):
        acc = acc + gate[:, j:j + 1] * ys_ref[j]
    y = _rms(acc, nf_ref[...])
    i = pl.program_id(0)

    @pl.when(i < prompt_tiles)
    def _():
        yp_ref[0] = y

    @pl.when(i >= prompt_tiles)
    def _():
        ys_out_ref[...] = y


def _combine(x1, ys_slots, gate, norm_final, batch, t_pad, seq):
    rows = x1.shape[0]
    tm = ROW_TILE
    assert t_pad - seq == tm, "one leading tile of padding + meta rows per sequence"
    per_seq = t_pad // tm
    prompt_tiles = batch * per_seq
    n_sample = rows - batch * t_pad
    nf = norm_final.reshape(1, D_MODEL)

    def prompt_map(i):
        j = jnp.minimum(i, prompt_tiles - 1)
        return (j // per_seq, jnp.maximum(j % per_seq - 1, 0), 0)

    return pl.pallas_call(
        functools.partial(_combine_kernel, prompt_tiles=prompt_tiles),
        grid=(rows // tm,),
        in_specs=[pl.BlockSpec((tm, D_MODEL), lambda i: (i, 0)),
                  pl.BlockSpec((TOP_K, tm, D_MODEL), lambda i: (0, i, 0)),
                  pl.BlockSpec((tm, TOP_K), lambda i: (i, 0)),
                  pl.BlockSpec((1, D_MODEL), lambda i: (0, 0))],
        out_specs=[pl.BlockSpec((1, tm, D_MODEL), prompt_map),
                   pl.BlockSpec((tm, D_MODEL), lambda i: (jnp.maximum(i - prompt_tiles, 0), 0))],
        out_shape=[jax.ShapeDtypeStruct((batch, seq, D_MODEL), F32),
                   jax.ShapeDtypeStruct((n_sample, D_MODEL), F32)],
        compiler_params=pltpu.CompilerParams(dimension_semantics=("arbitrary",),
                                             vmem_limit_bytes=VMEM_LIMIT),
        name="combine_norm",
    )(x1, ys_slots, gate, nf)


def kernel(x_prompt, x_sample, cache_k, cache_v, state_wkv, state_shift, page_table, meta, norm_mix, w_in, sb_bias, rw_mu, rw_w0, rw_decay_up, rw_a0, rw_aaa_up, rw_gate_up, rw_k_k, rw_k_a, rw_r_k, rw_ln_w, rw_ln_b, w_gate, b_gate, w_branch, w_out, norm_ffn, w_router, b_router, w1, b1, w2, b2, norm_final):
    assert w_in.shape[0] == 1, "single layer"
    batch, seq, _ = x_prompt.shape
    n_seq, n_new, _ = x_sample.shape
    t_real = N_META + seq
    t_pad = -(-t_real // ATT_TILE) * ATT_TILE
    pad = t_pad - t_real
    first_tok = pad + N_META
    assert t_pad % RWKV_CHUNK == 0 and (batch * t_pad) % ROW_TILE == 0 and (n_seq * n_new) % ROW_TILE == 0
    row_s = batch * t_pad

    meta_rows = jnp.broadcast_to(meta.astype(F32)[None], (batch, N_META, D_MODEL))
    xp = jnp.concatenate([jnp.zeros((batch, pad, D_MODEL), F32), meta_rows, x_prompt], axis=1)
    x_all = jnp.concatenate([xp.reshape(row_s, D_MODEL), x_sample.reshape(n_seq * n_new, D_MODEL)], axis=0)

    qb, k_all, v_all, kb, vb, p_all, gates = _projections(x_all, norm_mix[0], w_in[0], w_gate[0], b_gate[0])

    bias = sb_bias[0].astype(F32) * LOG2E
    attn_p = _prompt_attention(qb, kb, vb, bias, batch, t_pad, pad)
    attn_s = _paged_attention(qb, k_all, v_all, cache_k, cache_v, page_table, bias, row_s, n_seq, n_new)

    rw = (rw_mu[0], rw_w0[0], rw_decay_up[0], rw_a0[0], rw_aaa_up[0], rw_gate_up[0],
          rw_k_k[0], rw_k_a[0], rw_r_k[0], rw_ln_w[0], rw_ln_b[0])
    y_p, wkv_p = _rwkv(p_all, 0, batch, t_pad, RWKV_CHUNK,
                       jnp.zeros((batch, RW_PROJ), F32),
                       jnp.zeros((batch, HEADS, HEAD_DIM, HEAD_DIM), F32), rw)
    y_s, wkv_s = _rwkv(p_all, row_s, n_seq, n_new, n_new, state_shift[0], state_wkv[0], rw)

    attn = jnp.concatenate([attn_p, attn_s], axis=0)
    rw_out = jnp.concatenate([y_p, y_s], axis=0)
    x1, h2, top_idx, top_gate = _merge_and_route(x_all, attn, rw_out, gates, w_branch[0], w_out[0],
                                                 norm_ffn[0], w_router[0], b_router[0])

    n_rows = x_all.shape[0]
    row_tok, slot_pos, block_e, first, used = _route_tables(top_idx, n_rows)
    hs = _gather_rows(h2, row_tok)
    ys = _experts(hs, block_e, first, used, w1[0], b1[0], w2[0], b2[0])
    ys_slots = _gather_rows(ys, slot_pos.T.reshape(-1)).reshape(TOP_K, n_rows, D_MODEL)
    y_prompt, y_sample = _combine(x1, ys_slots, top_gate, norm_final, batch, t_pad, seq)
    y_sample = y_sample.reshape(n_seq, n_new, D_MODEL)

    def prompt_rows(a, lo):
        return a[:row_s].reshape(batch, t_pad, a.shape[-1])[:, lo:]

    k_p = prompt_rows(k_all, pad).reshape(1, batch, t_real, HEADS, HEAD_DIM)
    v_p = prompt_rows(v_all, pad).reshape(1, batch, t_real, HEADS, HEAD_DIM)
    shift_p = prompt_rows(p_all, t_pad - 1).reshape(1, batch, RW_PROJ)
    k_s = k_all[row_s:].reshape(1, n_seq, n_new, HEADS, HEAD_DIM)
    v_s = v_all[row_s:].reshape(1, n_seq, n_new, HEADS, HEAD_DIM)
    shift_s = p_all[row_s:].reshape(n_seq, n_new, RW_PROJ)[:, -1].reshape(1, n_seq, RW_PROJ)
    return (y_prompt, y_sample, k_p, v_p, wkv_p[None], shift_p, k_s, v_s, wkv_s[None], shift_s)
```

```python
import functools
import math

import numpy as np
import jax
import jax.numpy as jnp
from jax import lax
from jax.experimental import pallas as pl
from jax.experimental.pallas import tpu as pltpu
from jax.experimental.pallas import tpu_sc as plsc

F32 = jnp.float32
BF16 = jnp.bfloat16
HIGHEST = lax.Precision.HIGHEST

D_MODEL = 1024
N_META = 16
HEADS = 8
HEAD_DIM = 64
WIDTH = HEADS * HEAD_DIM
LOG2E = math.log2(math.e)
SB_SCALE2 = LOG2E / math.sqrt(HEAD_DIM)
DECAY_LORA = 64
AAA_LORA = 64
GATE_LORA = 128
RW_PROJ = 3 * WIDTH + DECAY_LORA + AAA_LORA + GATE_LORA
GN_EPS = 64e-5
N_EXPERTS = 32
TOP_K = 4
SWIGLU_LIMIT = 7.0
SWIGLU_ALPHA = 1.702
RMS_EPS = 1e-5

ROW_TILE = 256
ATT_TILE = 256
RWKV_CHUNK = 64
PAGES_PER_STEP = 16
MOE_TILE = 256
VMEM_LIMIT = 56 * 1024 * 1024

NT_DIMS = (((1,), (1,)), ((), ()))
TN_DIMS = (((0,), (0,)), ((), ()))
NN_DIMS = (((1,), (0,)), ((), ()))


def _dot(a, b):
    return jnp.dot(a, b, preferred_element_type=F32)


def _dotf(a, b, dims=NN_DIMS):
    return lax.dot_general(a, b, dims, precision=HIGHEST, preferred_element_type=F32)


def _split2(x):
    hi = x.astype(BF16)
    return hi, (x - hi.astype(F32)).astype(BF16)


def _dot3(a, b, dims=NN_DIMS):
    dg = lambda x, y: lax.dot_general(x, y, dims, preferred_element_type=F32)
    return dg(a[0], b[0]) + dg(a[0], b[1]) + dg(a[1], b[0])


def _split3(x):
    hi = x.astype(BF16)
    r1 = x - hi.astype(F32)
    mid = r1.astype(BF16)
    lo = (r1 - mid.astype(F32)).astype(BF16)
    return hi, mid, lo


def _dot_exact_rhs(x, m):
    hi, mid, lo = _split3(x)
    return _dot(hi, m) + _dot(mid, m) + _dot(lo, m)


def _dot_exact_lhs(m, x):
    hi, mid, lo = _split3(x)
    return _dot(m, hi) + _dot(m, mid) + _dot(m, lo)


def _neg_softplus(z):
    return jnp.minimum(-z, 0.0) - jnp.log1p(jnp.exp(-jnp.abs(z)))


def _rms(x, g):
    return x * lax.rsqrt(jnp.mean(x * x, axis=-1, keepdims=True) + RMS_EPS) * g


def _proj_kernel(x_ref, g_ref, wqkv_ref, wp_ref, wg_ref, bg_ref,
                 q_ref, k_ref, v_ref, kb_ref, vb_ref, p_ref, gate_ref):
    hb = _rms(x_ref[...], g_ref[...]).astype(BF16)
    qkv = _dot(hb, wqkv_ref[...])
    k = qkv[:, WIDTH:2 * WIDTH]
    v = qkv[:, 2 * WIDTH:]
    q_ref[...] = (qkv[:, :WIDTH] * SB_SCALE2).astype(BF16)
    k_ref[...] = k
    v_ref[...] = v
    kb_ref[...] = k.astype(BF16)
    vb_ref[...] = v.astype(BF16)
    p_ref[...] = _dot(hb, wp_ref[...])
    gate_ref[...] = jax.nn.sigmoid(_dot(hb, wg_ref[...]) + bg_ref[...])


def _projections(x, norm_g, w_in, w_gate, b_gate):
    rows = x.shape[0]
    tm = ROW_TILE
    wqkv = w_in[:, :3 * WIDTH].astype(BF16)
    wp = w_in[:, 3 * WIDTH:].astype(BF16)
    wg = w_gate.astype(BF16)
    row_spec = lambda n: pl.BlockSpec((tm, n), lambda i: (i, 0))
    full = lambda a: pl.BlockSpec(a.shape, lambda i: (0,) * a.ndim)
    g2 = norm_g.reshape(1, D_MODEL)
    bg = b_gate.reshape(1, 2 * D_MODEL)
    return pl.pallas_call(
        _proj_kernel,
        grid=(rows // tm,),
        in_specs=[row_spec(D_MODEL), full(g2), full(wqkv), full(wp), full(wg), full(bg)],
        out_specs=[row_spec(WIDTH)] * 5 + [row_spec(RW_PROJ), row_spec(2 * D_MODEL)],
        out_shape=[jax.ShapeDtypeStruct((rows, WIDTH), BF16),
                   jax.ShapeDtypeStruct((rows, WIDTH), F32),
                   jax.ShapeDtypeStruct((rows, WIDTH), F32),
                   jax.ShapeDtypeStruct((rows, WIDTH), BF16),
                   jax.ShapeDtypeStruct((rows, WIDTH), BF16),
                   jax.ShapeDtypeStruct((rows, RW_PROJ), F32),
                   jax.ShapeDtypeStruct((rows, 2 * D_MODEL), F32)],
        compiler_params=pltpu.CompilerParams(dimension_semantics=("arbitrary",),
                                             vmem_limit_bytes=VMEM_LIMIT),
        name="projections",
    )(x, g2, wqkv, wp, wg, bg)


def _stick_logs(z, visible):
    neg_abs = lax.bitcast_convert_type(
        lax.bitcast_convert_type(z, jnp.uint32) | jnp.uint32(0x80000000), F32)
    sp = jnp.maximum(z, 0.0) + jnp.log2(1.0 + jnp.exp2(neg_abs))
    log_beta = z - sp
    if visible is not None:
        sp = jnp.where(visible, sp, 0.0)
    hi = sp.astype(BF16)
    lo = (sp - hi.astype(F32)).astype(BF16)
    return jnp.concatenate([hi, lo], axis=1), log_beta, jnp.sum(sp, axis=-1, keepdims=True)


def _stick_weights(log_beta, after, visible):
    w = jnp.exp2(log_beta - after)
    if visible is not None:
        w = jnp.where(visible, w, 0.0)
    return w


def _suffix_matrix(n):
    j = np.arange(n)
    m = (j[:, None] > j[None, :]).astype(np.float32)
    return jnp.asarray(np.concatenate([m, m], axis=0), dtype=BF16)


def _prompt_attn_kernel(qi_ref, kt_ref, bias_ref, q_ref, k1_ref, v1_ref, k0_ref, v0_ref, m_ref, o_ref,
                        acc_ref, carry_ref, *, tile, pad):
    t = pl.program_id(1)
    i = qi_ref[t]
    j0 = kt_ref[t]
    first = j0 + 1 >= i

    @pl.when(first)
    def _():
        acc_ref[...] = jnp.zeros_like(acc_ref)
        carry_ref[...] = jnp.zeros_like(carry_ref)

    t_far = pl.num_programs(1) * 2 * tile
    subs = ((j0 + 1, k1_ref, v1_ref), (j0, k0_ref, v0_ref))

    def process(masked):
        q, m2 = q_ref[...], m_ref[...]
        head = lambda h: slice(h * HEAD_DIM, (h + 1) * HEAD_DIM)
        items = []
        for j, k_ref, v_ref in subs:
            visible = None
            if masked:
                q_pos = i * tile + lax.broadcasted_iota(jnp.int32, (tile, tile), 0)
                k_pos = j * tile + lax.broadcasted_iota(jnp.int32, (tile, tile), 1)
                visible = jnp.where(k_pos >= pad, k_pos, t_far) < q_pos
            k, v = k_ref[...], v_ref[...]
            items += [(h, k, v, visible) for h in range(HEADS)]
        scores, logs, afters = {}, {}, {}
        outs = [None] * HEADS
        for r in range(len(items) + 2):
            if r < len(items):
                h, k, _, _ = items[r]
                scores[r] = lax.dot_general(q[:, head(h)], k[:, head(h)], NT_DIMS,
                                            preferred_element_type=F32)
            if 0 <= r - 1 < len(items):
                h, _, _, visible = items[r - 1]
                hilo, log_beta, total = _stick_logs(scores.pop(r - 1) + bias_ref[h], visible)
                afters[r - 1] = _dot(hilo, m2)
                logs[r - 1] = (log_beta, total)
            if 0 <= r - 2 < len(items):
                h, _, v, visible = items[r - 2]
                log_beta, total = logs.pop(r - 2)
                w = _stick_weights(log_beta, afters.pop(r - 2), visible)
                carry = carry_ref[h]
                out = jnp.exp2(-carry) * _dot(w.astype(BF16), v[:, head(h)])
                outs[h] = out if outs[h] is None else outs[h] + out
                carry_ref[h] = carry + total
        acc_ref[...] += jnp.concatenate(outs, axis=1)

    needs_mask = jnp.logical_or(first, j0 == 0)
    pl.when(needs_mask)(lambda: process(True))
    pl.when(jnp.logical_not(needs_mask))(lambda: process(False))

    @pl.when(j0 == 0)
    def _():
        o_ref[...] = acc_ref[...].astype(o_ref.dtype)


def _prompt_attention(qb, kb, vb, bias2, batch, t_pad, pad):
    tile = ATT_TILE
    nq = t_pad // tile
    qi = np.concatenate([np.full(i // 2 + 1, i) for i in range(nq)]).astype(np.int32)
    kt = np.concatenate([2 * np.arange(i // 2, -1, -1) for i in range(nq)]).astype(np.int32)
    q_map = lambda b, t, qi, kt: (b * nq + qi[t], 0)
    hi_map = lambda b, t, qi, kt: (b * nq + jnp.minimum(kt[t] + 1, nq - 1), 0)
    lo_map = lambda b, t, qi, kt: (b * nq + kt[t], 0)
    grid_spec = pltpu.PrefetchScalarGridSpec(
        num_scalar_prefetch=2,
        grid=(batch, len(qi)),
        in_specs=[
            pl.BlockSpec(memory_space=pltpu.SMEM),
            pl.BlockSpec((tile, WIDTH), q_map),
            pl.BlockSpec((tile, WIDTH), hi_map),
            pl.BlockSpec((tile, WIDTH), hi_map),
            pl.BlockSpec((tile, WIDTH), lo_map),
            pl.BlockSpec((tile, WIDTH), lo_map),
            pl.BlockSpec((2 * tile, tile), lambda b, t, qi, kt: (0, 0)),
        ],
        out_specs=pl.BlockSpec((tile, WIDTH), q_map),
        scratch_shapes=[pltpu.VMEM((tile, WIDTH), F32), pltpu.VMEM((HEADS, tile, 1), F32)],
    )
    return pl.pallas_call(
        functools.partial(_prompt_attn_kernel, tile=tile, pad=pad),
        grid_spec=grid_spec,
        out_shape=jax.ShapeDtypeStruct((batch * t_pad, WIDTH), BF16),
        compiler_params=pltpu.CompilerParams(dimension_semantics=("arbitrary", "arbitrary"),
                                             vmem_limit_bytes=VMEM_LIMIT),
        name="prompt_attention",
    )(jnp.asarray(qi), jnp.asarray(kt), bias2, qb, kb, vb, kb, vb, _suffix_matrix(tile))


def _paged_attn_kernel(pt_ref, bias_ref, hmask_ref, m_ref, q_ref, kn_ref, vn_ref, *rest,
                       n_new, page, pages_per_step):
    page_refs = rest[:2 * pages_per_step]
    o_ref = rest[2 * pages_per_step]
    qbd_ref, acc_ref, carry_ref, knew_ref, vnew_ref = rest[2 * pages_per_step + 1:]
    s = pl.program_id(1)
    rows = HEADS * n_new

    def scores(keys, dim_major):
        dims = NN_DIMS if dim_major else NT_DIMS
        return (lax.dot_general(qbd_ref[...], keys.astype(BF16), dims, preferred_element_type=F32)
                + bias_ref[...])

    def logs(z, visible):
        hilo, log_beta, total = _stick_logs(z, visible)
        return _dot(hilo, m_ref[...]), log_beta, total

    def accumulate(after, log_beta, total, values, dim_major, visible):
        w = _stick_weights(log_beta, after, visible).astype(BF16)
        dims = NT_DIMS if dim_major else NN_DIMS
        out = lax.dot_general(w, values.astype(BF16), dims, preferred_element_type=F32)
        carry = carry_ref[...]
        acc_ref[...] += jnp.exp2(-carry) * out
        carry_ref[...] = carry + total

    @pl.when(s == 0)
    def _():
        q_rep = jnp.concatenate([q_ref[...]] * HEADS, axis=0)
        qbd_ref[...] = jnp.where(hmask_ref[...] > 0.0, q_rep, jnp.zeros_like(q_rep))
        acc_ref[...] = jnp.zeros_like(acc_ref)
        carry_ref[...] = jnp.zeros_like(carry_ref)
        knew_ref[...] = jnp.zeros_like(knew_ref)
        vnew_ref[...] = jnp.zeros_like(vnew_ref)
        knew_ref[0:n_new, :] = kn_ref[...]
        vnew_ref[0:n_new, :] = vn_ref[...]
        t_idx = lax.rem(lax.broadcasted_iota(jnp.int32, (rows, page), 0), n_new)
        c_idx = lax.broadcasted_iota(jnp.int32, (rows, page), 1)
        visible = c_idx < t_idx
        after, log_beta, total = logs(scores(knew_ref[...], False), visible)
        accumulate(after, log_beta, total, vnew_ref[...], False, visible)

    def channels(ref):
        return ref[0, 0].reshape(WIDTH, page)

    n = pages_per_step
    z, lg = {}, {}
    for r in range(n + 2):
        if r < n:
            z[r] = scores(channels(page_refs[2 * r]), True)
        if 0 <= r - 1 < n:
            lg[r - 1] = logs(z.pop(r - 1), None)
        if 0 <= r - 2 < n:
            accumulate(*lg.pop(r - 2), channels(page_refs[2 * (r - 2) + 1]), True, None)

    @pl.when(s == pl.num_programs(1) - 1)
    def _():
        own = acc_ref[...] * hmask_ref[...]
        o_ref[...] = own.reshape(HEADS, n_new, WIDTH).sum(axis=0).astype(o_ref.dtype)


def _paged_attention(qb, k_new, v_new, k_pool, v_pool, page_table, bias, row0, n_seq, n_new):
    n_pages = page_table.shape[1]
    page = k_pool.shape[2]
    k_pool = jnp.transpose(k_pool, (0, 1, 3, 4, 2))
    v_pool = jnp.transpose(v_pool, (0, 1, 3, 4, 2))
    pps = math.gcd(PAGES_PER_STEP, n_pages)
    steps = n_pages // pps
    rows = HEADS * n_new
    blk0 = row0 // n_new
    bias_col = jnp.repeat(bias.astype(F32), n_new).reshape(rows, 1)
    hmask = jnp.asarray((np.arange(rows)[:, None] // n_new == np.arange(WIDTH)[None, :] // HEAD_DIM)
                        .astype(np.float32))
    const = lambda a: pl.BlockSpec(a.shape, lambda b, s, pt: (0,) * a.ndim)
    new_spec = pl.BlockSpec((n_new, WIDTH), lambda b, s, pt: (blk0 + b, 0))

    def page_spec(p):
        return pl.BlockSpec((1, 1, HEADS, HEAD_DIM, page),
                            lambda b, s, pt: (0, pt[b, n_pages - 1 - (s * pps + p)], 0, 0, 0))

    page_specs, page_args = [], []
    for p in range(pps):
        page_specs += [page_spec(p), page_spec(p)]
        page_args += [k_pool, v_pool]
    m = _suffix_matrix(page)
    grid_spec = pltpu.PrefetchScalarGridSpec(
        num_scalar_prefetch=1,
        grid=(n_seq, steps),
        in_specs=[const(bias_col), const(hmask), const(m), new_spec, new_spec, new_spec] + page_specs,
        out_specs=pl.BlockSpec((n_new, WIDTH), lambda b, s, pt: (b, 0)),
        scratch_shapes=[pltpu.VMEM((rows, WIDTH), BF16), pltpu.VMEM((rows, WIDTH), F32),
                        pltpu.VMEM((rows, 1), F32), pltpu.VMEM((page, WIDTH), F32),
                        pltpu.VMEM((page, WIDTH), F32)],
    )
    return pl.pallas_call(
        functools.partial(_paged_attn_kernel, n_new=n_new, page=page, pages_per_step=pps),
        grid_spec=grid_spec,
        out_shape=jax.ShapeDtypeStruct((n_seq * n_new, WIDTH), BF16),
        compiler_params=pltpu.CompilerParams(dimension_semantics=("arbitrary", "arbitrary"),
                                             vmem_limit_bytes=VMEM_LIMIT),
        name="paged_attention",
    )(page_table, bias_col, hmask, m, qb, k_new, v_new, *page_args)


def _rwkv_kernel(p_ref, shift_ref, s0_ref, mu_ref, w0_ref, dup_ref, a0_ref, aup_ref, gup_ref,
                 kkw_ref, kaw_ref, rk_ref, lnw_ref, lnb_ref, hsum_ref,
                 y_ref, sout_ref, state_ref, prev_ref, *, chunk):
    c = pl.program_id(1)

    @pl.when(c == 0)
    def _():
        for h in range(HEADS):
            state_ref[h] = s0_ref[0, h].T
        prev_ref[...] = shift_ref[0]

    p = p_ref[...]
    row = lax.broadcasted_iota(jnp.int32, p.shape, 0)
    prev = jnp.where(row == 0, prev_ref[...], pltpu.roll(p, 1, 0))
    prev_ref[...] = p[chunk - 1:chunk, :]
    xm = p + (prev - p) * mu_ref[...]

    c1, c2, c3 = WIDTH, 2 * WIDTH, 3 * WIDTH
    c4 = c3 + DECAY_LORA
    c5 = c4 + AAA_LORA
    r, k, v = xm[:, :c1], xm[:, c1:c2], xm[:, c2:c3]
    w_lo, a_lo, g_lo = xm[:, c3:c4], xm[:, c4:c5], xm[:, c5:]
    log_w = _neg_softplus(-(w0_ref[...] + _dotf(jnp.tanh(w_lo), dup_ref[...]))) - 0.5
    log_decay = -jnp.exp(log_w)
    a = jax.nn.sigmoid(a0_ref[...] + _dotf(a_lo, aup_ref[...]))
    g = _dotf(jax.nn.sigmoid(g_lo), gup_ref[...])

    hsum = hsum_ref[...]
    kk = k * kkw_ref[...]
    kk = kk / jnp.maximum(jnp.sqrt(_dot_exact_rhs(kk * kk, hsum)), 1e-12)
    k = k * (1.0 + (a - 1.0) * kaw_ref[...])

    ti = lax.broadcasted_iota(jnp.int32, (chunk, chunk), 0)
    tj = lax.broadcasted_iota(jnp.int32, (chunk, chunk), 1)
    incl = ti >= tj
    strict = ti > tj
    cum = _dot_exact_lhs(jnp.where(incl, 1.0, 0.0).astype(BF16), log_decay)
    p_incl = jnp.exp(cum)
    p_inv = jnp.exp(-cum)
    p_end = p_incl[chunk - 1:chunk, :]
    a_t = -kk * jnp.exp(cum - log_decay)
    r_t = r * p_incl
    b_t = kk * a * p_inv
    k_t = k * p_inv
    b_e = b_t * p_end
    k_e = k_t * p_end
    eye = ti == tj

    heads = range(HEADS)
    per_head = lambda pair: [(pair[0][:, h * HEAD_DIM:(h + 1) * HEAD_DIM],
                              pair[1][:, h * HEAD_DIM:(h + 1) * HEAD_DIM]) for h in heads]
    split_all = lambda xs: [_split2(x) for x in xs]
    a_s, r_s, b_s, k_s, v_s = (per_head(_split2(x)) for x in (a_t, r_t, b_t, k_t, v))
    be_s, ke_s = per_head(_split2(b_e)), per_head(_split2(k_e))
    s0 = [state_ref[h] for h in heads]
    s0_s = split_all(s0)
    stack = lambda p, q: (jnp.concatenate([p[0], q[0]], axis=0), jnp.concatenate([p[1], q[1]], axis=0))
    stacked = chunk % 16 == 0
    if stacked:
        ar_s = [stack(a_s[h], r_s[h]) for h in heads]
        quad = [_dot3(ar_s[h], stack(b_s[h], k_s[h]), NT_DIMS) for h in heads]
        n_ab = [jnp.where(strict, quad[h][:chunk, :chunk], 0.0) for h in heads]
        a_ak = split_all([jnp.where(strict, quad[h][:chunk, chunk:], 0.0) for h in heads])
        col2 = lax.broadcasted_iota(jnp.int32, (chunk, 2 * chunk), 1)
        incl2 = (lax.broadcasted_iota(jnp.int32, (chunk, 2 * chunk), 0)
                 >= jnp.where(col2 >= chunk, col2 - chunk, col2))
        a_rbk = split_all([jnp.where(incl2, quad[h][chunk:, :], 0.0) for h in heads])
        on_state = [_dot3(ar_s[h], s0_s[h]) for h in heads]
        x_state = [t[:chunk] for t in on_state]
        y_state = [t[chunk:] for t in on_state]
    else:
        n_ab = [jnp.where(strict, _dot3(a_s[h], b_s[h], NT_DIMS), 0.0) for h in heads]
        a_ak = split_all([jnp.where(strict, _dot3(a_s[h], k_s[h], NT_DIMS), 0.0) for h in heads])
        a_rb = split_all([jnp.where(incl, _dot3(r_s[h], b_s[h], NT_DIMS), 0.0) for h in heads])
        a_rk = split_all([jnp.where(incl, _dot3(r_s[h], k_s[h], NT_DIMS), 0.0) for h in heads])
        x_state = [_dot3(a_s[h], s0_s[h]) for h in heads]
        y_state = [_dot3(r_s[h], s0_s[h]) for h in heads]
    inv = [jnp.where(eye, 1.0, 0.0) + n for n in n_ab]
    power_s = split_all(n_ab)
    for _ in range(max(int(math.log2(chunk)) - 1, 0)):
        power_s = split_all([_dot3(power_s[h], power_s[h]) for h in heads])
        inv_s = split_all(inv)
        inv = [inv[h] + _dot3(inv_s[h], power_s[h]) for h in heads]
    x_s = split_all([x_state[h] + _dot3(a_ak[h], v_s[h]) for h in heads])
    inv_s = split_all(inv)
    u_s = split_all([_dot3(inv_s[h], x_s[h]) for h in heads])
    diag = (lax.broadcasted_iota(jnp.int32, (HEAD_DIM, HEAD_DIM), 0)
            == lax.broadcasted_iota(jnp.int32, (HEAD_DIM, HEAD_DIM), 1))
    if stacked:
        uv_s = [stack(u_s[h], v_s[h]) for h in heads]
        ys = [y_state[h] + _dot3(a_rbk[h], uv_s[h]) for h in heads]
        grown = [_dot3(stack(be_s[h], ke_s[h]), uv_s[h], TN_DIMS) for h in heads]
    else:
        ys = [y_state[h] + _dot3(a_rb[h], u_s[h]) + _dot3(a_rk[h], v_s[h]) for h in heads]
        grown = [_dot3(be_s[h], u_s[h], TN_DIMS) + _dot3(ke_s[h], v_s[h], TN_DIMS) for h in heads]
    for h in heads:
        pe = p_end[:, h * HEAD_DIM:(h + 1) * HEAD_DIM]
        pe_col = jnp.sum(jnp.where(diag, pe, 0.0), axis=1, keepdims=True)
        state_ref[h] = pe_col * s0[h] + grown[h]
    y = jnp.concatenate(ys, axis=1)

    inv_n = 1.0 / HEAD_DIM
    mean = _dot_exact_rhs(y, hsum) * inv_n
    yc = y - mean
    var = _dot_exact_rhs(yc * yc, hsum) * inv_n
    yn = yc * lax.rsqrt(var + GN_EPS) * lnw_ref[...] + lnb_ref[...]
    bonus = _dot_exact_rhs(r * k * rk_ref[...], hsum) * v
    y_ref[...] = ((yn + bonus) * g).astype(y_ref.dtype)

    @pl.when(c == pl.num_programs(1) - 1)
    def _():
        for h in range(HEADS):
            sout_ref[0, h] = state_ref[h].T


def _rwkv(p_all, row0, n_seq, t_len, chunk, shift0, wkv0, rw):
    mu, w0, decay_up, a0, aaa_up, gate_up, k_k, k_a, r_k, ln_w, ln_b = rw
    n_chunks = t_len // chunk
    blk0 = row0 // chunk
    row = lambda a: a.reshape(1, -1).astype(F32)
    hsum = jnp.asarray((np.arange(WIDTH)[:, None] // HEAD_DIM == np.arange(WIDTH)[None, :] // HEAD_DIM)
                       .astype(np.float32), dtype=BF16)
    consts = [row(mu), row(w0), decay_up, row(a0), aaa_up, gate_up, row(k_k), row(k_a),
              row(r_k), row(ln_w), row(ln_b), hsum]
    const = lambda a: pl.BlockSpec(a.shape, lambda b, c: (0,) * a.ndim)
    shift3 = shift0.reshape(n_seq, 1, RW_PROJ)
    return pl.pallas_call(
        functools.partial(_rwkv_kernel, chunk=chunk),
        grid=(n_seq, n_chunks),
        in_specs=[pl.BlockSpec((chunk, RW_PROJ), lambda b, c: (blk0 + b * n_chunks + c, 0)),
                  pl.BlockSpec((1, 1, RW_PROJ), lambda b, c: (b, 0, 0)),
                  pl.BlockSpec((1, HEADS, HEAD_DIM, HEAD_DIM), lambda b, c: (b, 0, 0, 0))]
                 + [const(a) for a in consts],
        out_specs=[pl.BlockSpec((chunk, WIDTH), lambda b, c: (b * n_chunks + c, 0)),
                   pl.BlockSpec((1, HEADS, HEAD_DIM, HEAD_DIM), lambda b, c: (b, 0, 0, 0))],
        out_shape=[jax.ShapeDtypeStruct((n_seq * t_len, WIDTH), BF16),
                   jax.ShapeDtypeStruct((n_seq, HEADS, HEAD_DIM, HEAD_DIM), F32)],
        scratch_shapes=[pltpu.VMEM((HEADS, HEAD_DIM, HEAD_DIM), F32), pltpu.VMEM((1, RW_PROJ), F32)],
        compiler_params=pltpu.CompilerParams(dimension_semantics=("arbitrary", "arbitrary"),
                                             vmem_limit_bytes=VMEM_LIMIT),
        name=f"rwkv_chunk{chunk}",
    )(p_all, shift3, wkv0, *consts)


def _merge_kernel(x_ref, attn_ref, rw_ref, gate_ref, wb0_ref, wb1_ref, wo_ref, nf_ref, wr_ref, br_ref,
                  x1_ref, h2_ref, idx_ref, top_ref):
    gates = gate_ref[...]
    merged = (gates[:, :D_MODEL] * _dot(attn_ref[...], wb0_ref[...])
              + gates[:, D_MODEL:] * _dot(rw_ref[...], wb1_ref[...]))
    x1 = x_ref[...] + _dot(merged.astype(BF16), wo_ref[...])
    x1_ref[...] = x1
    h2 = _rms(x1, nf_ref[...])
    h2_ref[...] = h2
    logits = _dotf(h2, wr_ref[...]) + br_ref[...]
    lane = lax.broadcasted_iota(jnp.int32, logits.shape, 1)
    vals, idxs = [], []
    for _ in range(TOP_K):
        best = jnp.max(logits, axis=-1, keepdims=True)
        where = jnp.min(jnp.where(logits == best, lane, N_EXPERTS), axis=-1, keepdims=True)
        vals.append(best)
        idxs.append(where)
        logits = jnp.where(lane == where, -jnp.inf, logits)
    top = jnp.concatenate(vals, axis=1)
    e = jnp.exp(top - top[:, 0:1])
    top_ref[...] = e / jnp.sum(e, axis=-1, keepdims=True)
    idx_ref[...] = jnp.concatenate(idxs, axis=1)


def _merge_and_route(x, attn, rw_out, gates, w_branch, w_out, norm_ffn, w_router, b_router):
    rows = x.shape[0]
    tm = ROW_TILE
    wb0 = w_branch[0].astype(BF16)
    wb1 = w_branch[1].astype(BF16)
    wo = w_out.astype(BF16)
    nf = norm_ffn.reshape(1, D_MODEL)
    br = b_router.reshape(1, N_EXPERTS).astype(F32)
    wr = w_router.astype(F32)
    row_spec = lambda n: pl.BlockSpec((tm, n), lambda i: (i, 0))
    full = lambda a: pl.BlockSpec(a.shape, lambda i: (0,) * a.ndim)
    return pl.pallas_call(
        _merge_kernel,
        grid=(rows // tm,),
        in_specs=[row_spec(D_MODEL), row_spec(WIDTH), row_spec(WIDTH), row_spec(2 * D_MODEL),
                  full(wb0), full(wb1), full(wo), full(nf), full(wr), full(br)],
        out_specs=[row_spec(D_MODEL), row_spec(D_MODEL), row_spec(TOP_K), row_spec(TOP_K)],
        out_shape=[jax.ShapeDtypeStruct((rows, D_MODEL), F32),
                   jax.ShapeDtypeStruct((rows, D_MODEL), F32),
                   jax.ShapeDtypeStruct((rows, TOP_K), jnp.int32),
                   jax.ShapeDtypeStruct((rows, TOP_K), F32)],
        compiler_params=pltpu.CompilerParams(dimension_semantics=("arbitrary",),
                                             vmem_limit_bytes=VMEM_LIMIT),
        name="merge_route",
    )(x, attn, rw_out, gates, wb0, wb1, wo, nf, wr, br)


def _expert_kernel(be_ref, first_ref, used_ref, hs_ref, w1_ref, b1_ref, w2_ref, b2_ref, ys_ref,
                   w1b_ref, w2b_ref):
    i = pl.program_id(0)
    d_expert = w2_ref.shape[1]

    @pl.when(first_ref[i] == 1)
    def _():
        w1b_ref[...] = w1_ref[0].astype(BF16)
        w2b_ref[...] = w2_ref[0].astype(BF16)

    @pl.when(i < used_ref[0])
    def _():
        gu = _dot(hs_ref[...].astype(BF16), w1b_ref[...]) + b1_ref[0]
        gt = jnp.minimum(gu[:, :d_expert], SWIGLU_LIMIT)
        up = jnp.clip(gu[:, d_expert:], -SWIGLU_LIMIT, SWIGLU_LIMIT)
        act = (up + 1.0) * (gt * jax.nn.sigmoid(SWIGLU_ALPHA * gt))
        ys_ref[...] = _dot(act.astype(BF16), w2b_ref[...]) + b2_ref[0]

    @pl.when(i >= used_ref[0])
    def _():
        ys_ref[...] = jnp.zeros_like(ys_ref)


def _experts(hs, block_e, first, used, w1, b1, w2, b2):
    tmb = MOE_TILE
    n_blocks = hs.shape[0] // tmb
    d_expert = w2.shape[1]
    b1r = b1.reshape(N_EXPERTS, 1, 2 * d_expert)
    b2r = b2.reshape(N_EXPERTS, 1, D_MODEL)
    grid_spec = pltpu.PrefetchScalarGridSpec(
        num_scalar_prefetch=3,
        grid=(n_blocks,),
        in_specs=[pl.BlockSpec((tmb, D_MODEL), lambda i, be, fi, us: (i, 0)),
                  pl.BlockSpec((1, D_MODEL, 2 * d_expert), lambda i, be, fi, us: (be[i], 0, 0)),
                  pl.BlockSpec((1, 1, 2 * d_expert), lambda i, be, fi, us: (be[i], 0, 0)),
                  pl.BlockSpec((1, d_expert, D_MODEL), lambda i, be, fi, us: (be[i], 0, 0)),
                  pl.BlockSpec((1, 1, D_MODEL), lambda i, be, fi, us: (be[i], 0, 0))],
        out_specs=pl.BlockSpec((tmb, D_MODEL), lambda i, be, fi, us: (i, 0)),
        scratch_shapes=[pltpu.VMEM((D_MODEL, 2 * d_expert), BF16), pltpu.VMEM((d_expert, D_MODEL), BF16)],
    )
    return pl.pallas_call(
        _expert_kernel,
        grid_spec=grid_spec,
        out_shape=jax.ShapeDtypeStruct((n_blocks * tmb, D_MODEL), F32),
        compiler_params=pltpu.CompilerParams(dimension_semantics=("arbitrary",),
                                             vmem_limit_bytes=VMEM_LIMIT),
        name="experts",
    )(block_e, first, used, hs, w1, b1r, w2, b2r)


def _route_tables(top_idx, n_rows):
    tmb = MOE_TILE
    nk = n_rows * TOP_K
    n_blocks = nk // tmb + N_EXPERTS
    experts = jnp.arange(N_EXPERTS, dtype=jnp.int32)
    onehot = top_idx[:, :, None] == experts
    member = jnp.any(onehot, axis=1).astype(jnp.int32)
    counts = jnp.sum(member, axis=0)
    rank = jnp.cumsum(member, axis=0) - member
    padded = (counts + tmb - 1) // tmb * tmb
    pad_end = jnp.cumsum(padded)
    pad_start = pad_end - padded
    start = jnp.cumsum(counts) - counts
    slot_pos = jnp.sum(jnp.where(onehot, (pad_start + rank)[:, None, :], 0), axis=-1).astype(jnp.int32)
    order = jnp.argsort(top_idx.reshape(-1), stable=True).astype(jnp.int32)
    block_start = jnp.arange(n_blocks, dtype=jnp.int32) * tmb
    block_e = jnp.minimum(jnp.sum((pad_end[None, :] <= block_start[:, None]).astype(jnp.int32), axis=1),
                          N_EXPERTS - 1)
    row_e = jnp.repeat(block_e, tmb)
    in_group = jnp.arange(n_blocks * tmb, dtype=jnp.int32) - pad_start[row_e]
    valid = in_group < counts[row_e]
    sorted_slot = jnp.clip(start[row_e] + in_group, 0, nk - 1)
    row_tok = jnp.where(valid, order[sorted_slot] // TOP_K, 0).astype(jnp.int32)
    first = jnp.concatenate([jnp.ones((1,), jnp.int32),
                             (block_e[1:] != block_e[:-1]).astype(jnp.int32)])
    used = (pad_end[-1] // tmb).astype(jnp.int32).reshape(1)
    return row_tok, slot_pos, block_e.astype(jnp.int32), first, used


SC_CORES = 2
SC_SUBCORES = 16
SC_VMEM_ROWS = 112


def _gather_rows(table, idx):
    n, d = idx.shape[0], table.shape[1]
    workers = SC_CORES * SC_SUBCORES
    per_worker = n // workers
    assert n % workers == 0
    chunk = max(c for c in range(8, SC_VMEM_ROWS // 2 + 1, 8) if per_worker % (2 * c) == 0)
    mesh = plsc.VectorSubcoreMesh(core_axis_name="core", subcore_axis_name="subcore",
                                  num_cores=SC_CORES, num_subcores=SC_SUBCORES)
    buffer_types = [pltpu.VMEM((chunk,), jnp.int32), pltpu.VMEM((chunk, d), table.dtype),
                    pltpu.SemaphoreType.DMA, pltpu.SemaphoreType.DMA]

    @functools.partial(pl.kernel, mesh=mesh, out_type=jax.ShapeDtypeStruct((n, d), table.dtype),
                       scratch_types=buffer_types + buffer_types)
    def gather(table_hbm, idx_hbm, out_hbm, idx_a, rows_a, gsem_a, wsem_a, idx_b, rows_b, gsem_b, wsem_b):
        worker = lax.axis_index("subcore") * SC_CORES + lax.axis_index("core")

        def start_gather(c, idx_vmem, rows_vmem, sem):
            base = pl.multiple_of(worker * per_worker + c * chunk, 8)
            pltpu.sync_copy(idx_hbm.at[pl.ds(base, chunk)], idx_vmem)
            return pltpu.async_copy(table_hbm.at[idx_vmem], rows_vmem, sem)

        def start_write(c, rows_vmem, sem):
            base = pl.multiple_of(worker * per_worker + c * chunk, 8)
            return pltpu.async_copy(rows_vmem, out_hbm.at[pl.ds(base, chunk)], sem)

        @pl.loop(0, per_worker // (2 * chunk))
        def _(pair):
            gather_a = start_gather(2 * pair, idx_a, rows_a, gsem_a)
            gather_b = start_gather(2 * pair + 1, idx_b, rows_b, gsem_b)
            gather_a.wait()
            write_a = start_write(2 * pair, rows_a, wsem_a)
            gather_b.wait()
            write_b = start_write(2 * pair + 1, rows_b, wsem_b)
            write_a.wait()
            write_b.wait()

    return gather(table, idx)


def _combine_kernel(x1_ref, ys_ref, gate_ref, nf_ref, yp_ref, ys_out_ref, *, prompt_tiles):
    acc = x1_ref[...]
    gate = gate_ref[...]
    for j in range(TOP_K):
        acc = acc + gate[:, j:j + 1] * ys_ref[j]
    y = _rms(acc, nf_ref[...])
    i = pl.program_id(0)

    @pl.when(i < prompt_tiles)
    def _():
        yp_ref[0] = y

    @pl.when(i >= prompt_tiles)
    def _():
        ys_out_ref[...] = y


def _combine(x1, ys_slots, gate, norm_final, batch, t_pad, seq):
    rows = x1.shape[0]
    tm = ROW_TILE
    assert t_pad - seq == tm, "one leading tile of padding + meta rows per sequence"
    per_seq = t_pad // tm
    prompt_tiles = batch * per_seq
    n_sample = rows - batch * t_pad
    nf = norm_final.reshape(1, D_MODEL)

    def prompt_map(i):
        j = jnp.minimum(i, prompt_tiles - 1)
        return (j // per_seq, jnp.maximum(j % per_seq - 1, 0), 0)

    return pl.pallas_call(
        functools.partial(_combine_kernel, prompt_tiles=prompt_tiles),
        grid=(rows // tm,),
        in_specs=[pl.BlockSpec((tm, D_MODEL), lambda i: (i, 0)),
                  pl.BlockSpec((TOP_K, tm, D_MODEL), lambda i: (0, i, 0)),
                  pl.BlockSpec((tm, TOP_K), lambda i: (i, 0)),
                  pl.BlockSpec((1, D_MODEL), lambda i: (0, 0))],
        out_specs=[pl.BlockSpec((1, tm, D_MODEL), prompt_map),
                   pl.BlockSpec((tm, D_MODEL), lambda i: (jnp.maximum(i - prompt_tiles, 0), 0))],
        out_shape=[jax.ShapeDtypeStruct((batch, seq, D_MODEL), F32),
                   jax.ShapeDtypeStruct((n_sample, D_MODEL), F32)],
        compiler_params=pltpu.CompilerParams(dimension_semantics=("arbitrary",),
                                             vmem_limit_bytes=VMEM_LIMIT),
        name="combine_norm",
    )(x1, ys_slots, gate, nf)


def kernel(x_prompt, x_sample, cache_k, cache_v, state_wkv, state_shift, page_table, meta, norm_mix, w_in, sb_bias, rw_mu, rw_w0, rw_decay_up, rw_a0, rw_aaa_up, rw_gate_up, rw_k_k, rw_k_a, rw_r_k, rw_ln_w, rw_ln_b, w_gate, b_gate, w_branch, w_out, norm_ffn, w_router, b_router, w1, b1, w2, b2, norm_final):
    assert w_in.shape[0] == 1, "single layer"
    batch, seq, _ = x_prompt.shape
    n_seq, n_new, _ = x_sample.shape
    t_real = N_META + seq
    t_pad = -(-t_real // ATT_TILE) * ATT_TILE
    pad = t_pad - t_real
    first_tok = pad + N_META
    assert t_pad % RWKV_CHUNK == 0 and (batch * t_pad) % ROW_TILE == 0 and (n_seq * n_new) % ROW_TILE == 0
    row_s = batch * t_pad

    meta_rows = jnp.broadcast_to(meta.astype(F32)[None], (batch, N_META, D_MODEL))
    xp = jnp.concatenate([jnp.zeros((batch, pad, D_MODEL), F32), meta_rows, x_prompt], axis=1)
    x_all = jnp.concatenate([xp.reshape(row_s, D_MODEL), x_sample.reshape(n_seq * n_new, D_MODEL)], axis=0)

    qb, k_all, v_all, kb, vb, p_all, gates = _projections(x_all, norm_mix[0], w_in[0], w_gate[0], b_gate[0])

    bias = sb_bias[0].astype(F32) * LOG2E
    attn_p = _prompt_attention(qb, kb, vb, bias, batch, t_pad, pad)
    attn_s = _paged_attention(qb, k_all, v_all, cache_k, cache_v, page_table, bias, row_s, n_seq, n_new)

    rw = (rw_mu[0], rw_w0[0], rw_decay_up[0], rw_a0[0], rw_aaa_up[0], rw_gate_up[0],
          rw_k_k[0], rw_k_a[0], rw_r_k[0], rw_ln_w[0], rw_ln_b[0])
    y_p, wkv_p = _rwkv(p_all, 0, batch, t_pad, RWKV_CHUNK,
                       jnp.zeros((batch, RW_PROJ), F32),
                       jnp.zeros((batch, HEADS, HEAD_DIM, HEAD_DIM), F32), rw)
    y_s, wkv_s = _rwkv(p_all, row_s, n_seq, n_new, n_new, state_shift[0], state_wkv[0], rw)

    attn = jnp.concatenate([attn_p, attn_s], axis=0)
    rw_out = jnp.concatenate([y_p, y_s], axis=0)
    x1, h2, top_idx, top_gate = _merge_and_route(x_all, attn, rw_out, gates, w_branch[0], w_out[0],
                                                 norm_ffn[0], w_router[0], b_router[0])

    n_rows = x_all.shape[0]
    row_tok, slot_pos, block_e, first, used = _route_tables(top_idx, n_rows)
    hs = _gather_rows(h2, row_tok)
    ys = _experts(hs, block_e, first, used, w1[0], b1[0], w2[0], b2[0])
    ys_slots = _gather_rows(ys, slot_pos.T.reshape(-1)).reshape(TOP_K, n_rows, D_MODEL)
    y_prompt, y_sample = _combine(x1, ys_slots, top_gate, norm_final, batch, t_pad, seq)
    y_sample = y_sample.reshape(n_seq, n_new, D_MODEL)

    def prompt_rows(a, lo):
        return a[:row_s].reshape(batch, t_pad, a.shape[-1])[:, lo:]

    k_p = prompt_rows(k_all, pad).reshape(1, batch, t_real, HEADS, HEAD_DIM)
    v_p = prompt_rows(v_all, pad).reshape(1, batch, t_real, HEADS, HEAD_DIM)
    shift_p = prompt_rows(p_all, t_pad - 1).reshape(1, batch, RW_PROJ)
    k_s = k_all[row_s:].reshape(1, n_seq, n_new, HEADS, HEAD_DIM)
    v_s = v_all[row_s:].reshape(1, n_seq, n_new, HEADS, HEAD_DIM)
    shift_s = p_all[row_s:].reshape(n_seq, n_new, RW_PROJ)[:, -1].reshape(1, n_seq, RW_PROJ)
    return (y_prompt, y_sample, k_p, v_p, wkv_p[None], shift_p, k_s, v_s, wkv_s[None], shift_s)
```

```python
import functools
import math

import numpy as np
import jax
import jax.numpy as jnp
from jax import lax
from jax.experimental import pallas as pl
from jax.experimental.pallas import tpu as pltpu
from jax.experimental.pallas import tpu_sc as plsc

F32 = jnp.float32
BF16 = jnp.bfloat16
HIGHEST = lax.Precision.HIGHEST

D_MODEL = 1024
N_META = 16
HEADS = 8
HEAD_DIM = 64
WIDTH = HEADS * HEAD_DIM
LOG2E = math.log2(math.e)
SB_SCALE2 = LOG2E / math.sqrt(HEAD_DIM)
DECAY_LORA = 64
AAA_LORA = 64
GATE_LORA = 128
RW_PROJ = 3 * WIDTH + DECAY_LORA + AAA_LORA + GATE_LORA
GN_EPS = 64e-5
N_EXPERTS = 32
TOP_K = 4
SWIGLU_LIMIT = 7.0
SWIGLU_ALPHA = 1.702
RMS_EPS = 1e-5

ROW_TILE = 256
ATT_TILE = 256
RWKV_CHUNK = 64
PAGES_PER_STEP = 16
MOE_TILE = 256
VMEM_LIMIT = 56 * 1024 * 1024

NT_DIMS = (((1,), (1,)), ((), ()))
TN_DIMS = (((0,), (0,)), ((), ()))
NN_DIMS = (((1,), (0,)), ((), ()))


def _dot(a, b):
    return jnp.dot(a, b, preferred_element_type=F32)


def _dotf(a, b, dims=NN_DIMS):
    return lax.dot_general(a, b, dims, precision=HIGHEST, preferred_element_type=F32)


def _split2(x):
    hi = x.astype(BF16)
    return hi, (x - hi.astype(F32)).astype(BF16)


def _dot3(a, b, dims=NN_DIMS):
    dg = lambda x, y: lax.dot_general(x, y, dims, preferred_element_type=F32)
    return dg(a[0], b[0]) + dg(a[0], b[1]) + dg(a[1], b[0])


def _split3(x):
    hi = x.astype(BF16)
    r1 = x - hi.astype(F32)
    mid = r1.astype(BF16)
    lo = (r1 - mid.astype(F32)).astype(BF16)
    return hi, mid, lo


def _dot_exact_rhs(x, m):
    hi, mid, lo = _split3(x)
    return _dot(hi, m) + _dot(mid, m) + _dot(lo, m)


def _dot_exact_lhs(m, x):
    hi, mid, lo = _split3(x)
    return _dot(m, hi) + _dot(m, mid) + _dot(m, lo)


def _neg_softplus(z):
    return jnp.minimum(-z, 0.0) - jnp.log1p(jnp.exp(-jnp.abs(z)))


def _rms(x, g):
    return x * lax.rsqrt(jnp.mean(x * x, axis=-1, keepdims=True) + RMS_EPS) * g


def _proj_kernel(xp_ref, xs_ref, meta_ref, g_ref, wqkv_ref, wp_ref, wg_ref, bg_ref,
                 x_ref, q_ref, k_ref, v_ref, kb_ref, vb_ref, p_ref, gate_ref, *, per_seq, prompt_tiles):
    i = pl.program_id(0)
    tm = x_ref.shape[0]
    leading = jnp.logical_and(i < prompt_tiles, lax.rem(i, per_seq) == 0)

    @pl.when(leading)
    def _():
        x_ref[0:tm - N_META, :] = jnp.zeros((tm - N_META, D_MODEL), F32)
        x_ref[tm - N_META:tm, :] = meta_ref[...]

    @pl.when(jnp.logical_and(i < prompt_tiles, jnp.logical_not(leading)))
    def _():
        x_ref[...] = xp_ref[0]

    @pl.when(i >= prompt_tiles)
    def _():
        x_ref[...] = xs_ref[...]

    hb = _rms(x_ref[...], g_ref[...]).astype(BF16)
    qkv = _dot(hb, wqkv_ref[...])
    k = qkv[:, WIDTH:2 * WIDTH]
    v = qkv[:, 2 * WIDTH:]
    q_ref[...] = (qkv[:, :WIDTH] * SB_SCALE2).astype(BF16)
    k_ref[...] = k
    v_ref[...] = v
    kb_ref[...] = k.astype(BF16)
    vb_ref[...] = v.astype(BF16)
    p_ref[...] = _dot(hb, wp_ref[...])
    gate_ref[...] = jax.nn.sigmoid(_dot(hb, wg_ref[...]) + bg_ref[...])


def _projections(x_prompt, x_sample, meta, t_pad, norm_g, w_in, w_gate, b_gate):
    batch, seq, _ = x_prompt.shape
    tm = ROW_TILE
    assert t_pad - seq == tm and x_sample.shape[0] % tm == 0
    per_seq = t_pad // tm
    prompt_tiles = batch * per_seq
    rows = batch * t_pad + x_sample.shape[0]
    wqkv = w_in[:, :3 * WIDTH].astype(BF16)
    wp = w_in[:, 3 * WIDTH:].astype(BF16)
    wg = w_gate.astype(BF16)
    row_spec = lambda n: pl.BlockSpec((tm, n), lambda i: (i, 0))
    full = lambda a: pl.BlockSpec(a.shape, lambda i: (0,) * a.ndim)
    g2 = norm_g.reshape(1, D_MODEL)
    bg = b_gate.reshape(1, 2 * D_MODEL)

    def prompt_map(i):
        j = jnp.minimum(i, prompt_tiles - 1)
        return (j // per_seq, jnp.maximum(j % per_seq - 1, 0), 0)

    return pl.pallas_call(
        functools.partial(_proj_kernel, per_seq=per_seq, prompt_tiles=prompt_tiles),
        grid=(rows // tm,),
        in_specs=[pl.BlockSpec((1, tm, D_MODEL), prompt_map),
                  pl.BlockSpec((tm, D_MODEL), lambda i: (jnp.maximum(i - prompt_tiles, 0), 0)),
                  full(meta), full(g2), full(wqkv), full(wp), full(wg), full(bg)],
        out_specs=[row_spec(D_MODEL)] + [row_spec(WIDTH)] * 5 + [row_spec(RW_PROJ), row_spec(2 * D_MODEL)],
        out_shape=[jax.ShapeDtypeStruct((rows, D_MODEL), F32),
                   jax.ShapeDtypeStruct((rows, WIDTH), BF16),
                   jax.ShapeDtypeStruct((rows, WIDTH), F32),
                   jax.ShapeDtypeStruct((rows, WIDTH), F32),
                   jax.ShapeDtypeStruct((rows, WIDTH), BF16),
                   jax.ShapeDtypeStruct((rows, WIDTH), BF16),
                   jax.ShapeDtypeStruct((rows, RW_PROJ), F32),
                   jax.ShapeDtypeStruct((rows, 2 * D_MODEL), F32)],
        compiler_params=pltpu.CompilerParams(dimension_semantics=("arbitrary",),
                                             vmem_limit_bytes=VMEM_LIMIT),
        name="projections",
    )(x_prompt, x_sample, meta.astype(F32), g2, wqkv, wp, wg, bg)


def _stick_logs(z, visible):
    neg_abs = lax.bitcast_convert_type(
        lax.bitcast_convert_type(z, jnp.uint32) | jnp.uint32(0x80000000), F32)
    sp = jnp.maximum(z, 0.0) + jnp.log2(1.0 + jnp.exp2(neg_abs))
    log_beta = z - sp
    if visible is not None:
        sp = jnp.where(visible, sp, 0.0)
    hi = sp.astype(BF16)
    lo = (sp - hi.astype(F32)).astype(BF16)
    return jnp.concatenate([hi, lo], axis=1), log_beta, jnp.sum(sp, axis=-1, keepdims=True)


def _stick_weights(log_beta, after, visible):
    w = jnp.exp2(log_beta - after)
    if visible is not None:
        w = jnp.where(visible, w, 0.0)
    return w


def _suffix_matrix(n):
    j = np.arange(n)
    m = (j[:, None] > j[None, :]).astype(np.float32)
    return jnp.asarray(np.concatenate([m, m], axis=0), dtype=BF16)


def _prompt_attn_kernel(qi_ref, kt_ref, bias_ref, q_ref, k1_ref, v1_ref, k0_ref, v0_ref, m_ref, o_ref,
                        acc_ref, carry_ref, *, tile, pad):
    t = pl.program_id(1)
    i = qi_ref[t]
    j0 = kt_ref[t]
    first = j0 + 1 >= i

    @pl.when(first)
    def _():
        acc_ref[...] = jnp.zeros_like(acc_ref)
        carry_ref[...] = jnp.zeros_like(carry_ref)

    t_far = pl.num_programs(1) * 2 * tile
    subs = ((j0 + 1, k1_ref, v1_ref), (j0, k0_ref, v0_ref))

    def process(masked):
        q, m2 = q_ref[...], m_ref[...]
        head = lambda h: slice(h * HEAD_DIM, (h + 1) * HEAD_DIM)
        items = []
        for j, k_ref, v_ref in subs:
            visible = None
            if masked:
                q_pos = i * tile + lax.broadcasted_iota(jnp.int32, (tile, tile), 0)
                k_pos = j * tile + lax.broadcasted_iota(jnp.int32, (tile, tile), 1)
                visible = jnp.where(k_pos >= pad, k_pos, t_far) < q_pos
            k, v = k_ref[...], v_ref[...]
            items += [(h, k, v, visible) for h in range(HEADS)]
        scores, logs, afters = {}, {}, {}
        outs = [None] * HEADS
        for r in range(len(items) + 2):
            if r < len(items):
                h, k, _, _ = items[r]
                scores[r] = lax.dot_general(q[:, head(h)], k[:, head(h)], NT_DIMS,
                                            preferred_element_type=F32)
            if 0 <= r - 1 < len(items):
                h, _, _, visible = items[r - 1]
                hilo, log_beta, total = _stick_logs(scores.pop(r - 1) + bias_ref[h], visible)
                afters[r - 1] = _dot(hilo, m2)
                logs[r - 1] = (log_beta, total)
            if 0 <= r - 2 < len(items):
                h, _, v, visible = items[r - 2]
                log_beta, total = logs.pop(r - 2)
                w = _stick_weights(log_beta, afters.pop(r - 2), visible)
                carry = carry_ref[h]
                out = jnp.exp2(-carry) * _dot(w.astype(BF16), v[:, head(h)])
                outs[h] = out if outs[h] is None else outs[h] + out
                carry_ref[h] = carry + total
        acc_ref[...] += jnp.concatenate(outs, axis=1)

    needs_mask = jnp.logical_or(first, j0 == 0)
    pl.when(needs_mask)(lambda: process(True))
    pl.when(jnp.logical_not(needs_mask))(lambda: process(False))

    @pl.when(j0 == 0)
    def _():
        o_ref[...] = acc_ref[...].astype(o_ref.dtype)


def _prompt_attention(qb, kb, vb, bias2, batch, t_pad, pad):
    tile = ATT_TILE
    nq = t_pad // tile
    qi = np.concatenate([np.full(i // 2 + 1, i) for i in range(nq)]).astype(np.int32)
    kt = np.concatenate([2 * np.arange(i // 2, -1, -1) for i in range(nq)]).astype(np.int32)
    q_map = lambda b, t, qi, kt: (b * nq + qi[t], 0)
    hi_map = lambda b, t, qi, kt: (b * nq + jnp.minimum(kt[t] + 1, nq - 1), 0)
    lo_map = lambda b, t, qi, kt: (b * nq + kt[t], 0)
    grid_spec = pltpu.PrefetchScalarGridSpec(
        num_scalar_prefetch=2,
        grid=(batch, len(qi)),
        in_specs=[
            pl.BlockSpec(memory_space=pltpu.SMEM),
            pl.BlockSpec((tile, WIDTH), q_map),
            pl.BlockSpec((tile, WIDTH), hi_map),
            pl.BlockSpec((tile, WIDTH), hi_map),
            pl.BlockSpec((tile, WIDTH), lo_map),
            pl.BlockSpec((tile, WIDTH), lo_map),
            pl.BlockSpec((2 * tile, tile), lambda b, t, qi, kt: (0, 0)),
        ],
        out_specs=pl.BlockSpec((tile, WIDTH), q_map),
        scratch_shapes=[pltpu.VMEM((tile, WIDTH), F32), pltpu.VMEM((HEADS, tile, 1), F32)],
    )
    return pl.pallas_call(
        functools.partial(_prompt_attn_kernel, tile=tile, pad=pad),
        grid_spec=grid_spec,
        out_shape=jax.ShapeDtypeStruct((batch * t_pad, WIDTH), BF16),
        compiler_params=pltpu.CompilerParams(dimension_semantics=("arbitrary", "arbitrary"),
                                             vmem_limit_bytes=VMEM_LIMIT),
        name="prompt_attention",
    )(jnp.asarray(qi), jnp.asarray(kt), bias2, qb, kb, vb, kb, vb, _suffix_matrix(tile))


def _paged_attn_kernel(pt_ref, bias_ref, hmask_ref, m_ref, q_ref, kn_ref, vn_ref, *rest,
                       n_new, page, pages_per_step):
    page_refs = rest[:2 * pages_per_step]
    o_ref = rest[2 * pages_per_step]
    qbd_ref, acc_ref, carry_ref, knew_ref, vnew_ref = rest[2 * pages_per_step + 1:]
    s = pl.program_id(1)
    rows = HEADS * n_new

    def scores(keys, dim_major):
        dims = NN_DIMS if dim_major else NT_DIMS
        return (lax.dot_general(qbd_ref[...], keys.astype(BF16), dims, preferred_element_type=F32)
                + bias_ref[...])

    def logs(z, visible):
        hilo, log_beta, total = _stick_logs(z, visible)
        return _dot(hilo, m_ref[...]), log_beta, total

    def accumulate(after, log_beta, total, values, dim_major, visible):
        w = _stick_weights(log_beta, after, visible).astype(BF16)
        dims = NT_DIMS if dim_major else NN_DIMS
        out = lax.dot_general(w, values.astype(BF16), dims, preferred_element_type=F32)
        carry = carry_ref[...]
        acc_ref[...] += jnp.exp2(-carry) * out
        carry_ref[...] = carry + total

    @pl.when(s == 0)
    def _():
        q_rep = jnp.concatenate([q_ref[...]] * HEADS, axis=0)
        qbd_ref[...] = jnp.where(hmask_ref[...] > 0.0, q_rep, jnp.zeros_like(q_rep))
        acc_ref[...] = jnp.zeros_like(acc_ref)
        carry_ref[...] = jnp.zeros_like(carry_ref)
        knew_ref[...] = jnp.zeros_like(knew_ref)
        vnew_ref[...] = jnp.zeros_like(vnew_ref)
        knew_ref[0:n_new, :] = kn_ref[...]
        vnew_ref[0:n_new, :] = vn_ref[...]
        t_idx = lax.rem(lax.broadcasted_iota(jnp.int32, (rows, page), 0), n_new)
        c_idx = lax.broadcasted_iota(jnp.int32, (rows, page), 1)
        visible = c_idx < t_idx
        after, log_beta, total = logs(scores(knew_ref[...], False), visible)
        accumulate(after, log_beta, total, vnew_ref[...], False, visible)

    def channels(ref):
        return ref[0, 0].reshape(WIDTH, page)

    n = pages_per_step
    z, lg = {}, {}
    for r in range(n + 2):
        if r < n:
            z[r] = scores(channels(page_refs[2 * r]), True)
        if 0 <= r - 1 < n:
            lg[r - 1] = logs(z.pop(r - 1), None)
        if 0 <= r - 2 < n:
            accumulate(*lg.pop(r - 2), channels(page_refs[2 * (r - 2) + 1]), True, None)

    @pl.when(s == pl.num_programs(1) - 1)
    def _():
        own = acc_ref[...] * hmask_ref[...]
        o_ref[...] = own.reshape(HEADS, n_new, WIDTH).sum(axis=0).astype(o_ref.dtype)


def _paged_attention(qb, k_new, v_new, k_pool, v_pool, page_table, bias, row0, n_seq, n_new):
    n_pages = page_table.shape[1]
    page = k_pool.shape[2]
    k_pool = jnp.transpose(k_pool, (0, 1, 3, 4, 2))
    v_pool = jnp.transpose(v_pool, (0, 1, 3, 4, 2))
    pps = math.gcd(PAGES_PER_STEP, n_pages)
    steps = n_pages // pps
    rows = HEADS * n_new
    blk0 = row0 // n_new
    bias_col = jnp.repeat(bias.astype(F32), n_new).reshape(rows, 1)
    hmask = jnp.asarray((np.arange(rows)[:, None] // n_new == np.arange(WIDTH)[None, :] // HEAD_DIM)
                        .astype(np.float32))
    const = lambda a: pl.BlockSpec(a.shape, lambda b, s, pt: (0,) * a.ndim)
    new_spec = pl.BlockSpec((n_new, WIDTH), lambda b, s, pt: (blk0 + b, 0))

    def page_spec(p):
        return pl.BlockSpec((1, 1, HEADS, HEAD_DIM, page),
                            lambda b, s, pt: (0, pt[b, n_pages - 1 - (s * pps + p)], 0, 0, 0))

    page_specs, page_args = [], []
    for p in range(pps):
        page_specs += [page_spec(p), page_spec(p)]
        page_args += [k_pool, v_pool]
    m = _suffix_matrix(page)
    grid_spec = pltpu.PrefetchScalarGridSpec(
        num_scalar_prefetch=1,
        grid=(n_seq, steps),
        in_specs=[const(bias_col), const(hmask), const(m), new_spec, new_spec, new_spec] + page_specs,
        out_specs=pl.BlockSpec((n_new, WIDTH), lambda b, s, pt: (b, 0)),
        scratch_shapes=[pltpu.VMEM((rows, WIDTH), BF16), pltpu.VMEM((rows, WIDTH), F32),
                        pltpu.VMEM((rows, 1), F32), pltpu.VMEM((page, WIDTH), F32),
                        pltpu.VMEM((page, WIDTH), F32)],
    )
    return pl.pallas_call(
        functools.partial(_paged_attn_kernel, n_new=n_new, page=page, pages_per_step=pps),
        grid_spec=grid_spec,
        out_shape=jax.ShapeDtypeStruct((n_seq * n_new, WIDTH), BF16),
        compiler_params=pltpu.CompilerParams(dimension_semantics=("arbitrary", "arbitrary"),
                                             vmem_limit_bytes=VMEM_LIMIT),
        name="paged_attention",
    )(page_table, bias_col, hmask, m, qb, k_new, v_new, *page_args)


def _rwkv_kernel(p_ref, shift_ref, s0_ref, mu_ref, w0_ref, dup_ref, a0_ref, aup_ref, gup_ref,
                 kkw_ref, kaw_ref, rk_ref, lnw_ref, lnb_ref, hsum_ref,
                 y_ref, sout_ref, state_ref, prev_ref, *, chunk):
    c = pl.program_id(1)

    @pl.when(c == 0)
    def _():
        for h in range(HEADS):
            state_ref[h] = s0_ref[0, h].T
        prev_ref[...] = shift_ref[0]

    p = p_ref[...]
    row = lax.broadcasted_iota(jnp.int32, p.shape, 0)
    prev = jnp.where(row == 0, prev_ref[...], pltpu.roll(p, 1, 0))
    prev_ref[...] = p[chunk - 1:chunk, :]
    xm = p + (prev - p) * mu_ref[...]

    c1, c2, c3 = WIDTH, 2 * WIDTH, 3 * WIDTH
    c4 = c3 + DECAY_LORA
    c5 = c4 + AAA_LORA
    r, k, v = xm[:, :c1], xm[:, c1:c2], xm[:, c2:c3]
    w_lo, a_lo, g_lo = xm[:, c3:c4], xm[:, c4:c5], xm[:, c5:]
    log_w = _neg_softplus(-(w0_ref[...] + _dotf(jnp.tanh(w_lo), dup_ref[...]))) - 0.5
    log_decay = -jnp.exp(log_w)
    a = jax.nn.sigmoid(a0_ref[...] + _dotf(a_lo, aup_ref[...]))
    g = _dotf(jax.nn.sigmoid(g_lo), gup_ref[...])

    hsum = hsum_ref[...]
    kk = k * kkw_ref[...]
    kk = kk / jnp.maximum(jnp.sqrt(_dot_exact_rhs(kk * kk, hsum)), 1e-12)
    k = k * (1.0 + (a - 1.0) * kaw_ref[...])

    ti = lax.broadcasted_iota(jnp.int32, (chunk, chunk), 0)
    tj = lax.broadcasted_iota(jnp.int32, (chunk, chunk), 1)
    incl = ti >= tj
    strict = ti > tj
    cum = _dot_exact_lhs(jnp.where(incl, 1.0, 0.0).astype(BF16), log_decay)
    p_incl = jnp.exp(cum)
    p_inv = jnp.exp(-cum)
    p_end = p_incl[chunk - 1:chunk, :]
    a_t = -kk * jnp.exp(cum - log_decay)
    r_t = r * p_incl
    b_t = kk * a * p_inv
    k_t = k * p_inv
    b_e = b_t * p_end
    k_e = k_t * p_end
    eye = ti == tj

    heads = range(HEADS)
    per_head = lambda pair: [(pair[0][:, h * HEAD_DIM:(h + 1) * HEAD_DIM],
                              pair[1][:, h * HEAD_DIM:(h + 1) * HEAD_DIM]) for h in heads]
    split_all = lambda xs: [_split2(x) for x in xs]
    a_s, r_s, b_s, k_s, v_s = (per_head(_split2(x)) for x in (a_t, r_t, b_t, k_t, v))
    be_s, ke_s = per_head(_split2(b_e)), per_head(_split2(k_e))
    s0 = [state_ref[h] for h in heads]
    s0_s = split_all(s0)
    stack = lambda p, q: (jnp.concatenate([p[0], q[0]], axis=0), jnp.concatenate([p[1], q[1]], axis=0))
    stacked = chunk % 16 == 0
    if stacked:
        ar_s = [stack(a_s[h], r_s[h]) for h in heads]
        quad = [_dot3(ar_s[h], stack(b_s[h], k_s[h]), NT_DIMS) for h in heads]
        n_ab = [jnp.where(strict, quad[h][:chunk, :chunk], 0.0) for h in heads]
        a_ak = split_all([jnp.where(strict, quad[h][:chunk, chunk:], 0.0) for h in heads])
        col2 = lax.broadcasted_iota(jnp.int32, (chunk, 2 * chunk), 1)
        incl2 = (lax.broadcasted_iota(jnp.int32, (chunk, 2 * chunk), 0)
                 >= jnp.where(col2 >= chunk, col2 - chunk, col2))
        a_rbk = split_all([jnp.where(incl2, quad[h][chunk:, :], 0.0) for h in heads])
        on_state = [_dot3(ar_s[h], s0_s[h]) for h in heads]
        x_state = [t[:chunk] for t in on_state]
        y_state = [t[chunk:] for t in on_state]
    else:
        n_ab = [jnp.where(strict, _dot3(a_s[h], b_s[h], NT_DIMS), 0.0) for h in heads]
        a_ak = split_all([jnp.where(strict, _dot3(a_s[h], k_s[h], NT_DIMS), 0.0) for h in heads])
        a_rb = split_all([jnp.where(incl, _dot3(r_s[h], b_s[h], NT_DIMS), 0.0) for h in heads])
        a_rk = split_all([jnp.where(incl, _dot3(r_s[h], k_s[h], NT_DIMS), 0.0) for h in heads])
        x_state = [_dot3(a_s[h], s0_s[h]) for h in heads]
        y_state = [_dot3(r_s[h], s0_s[h]) for h in heads]
    inv = [jnp.where(eye, 1.0, 0.0) + n for n in n_ab]
    power_s = split_all(n_ab)
    for _ in range(max(int(math.log2(chunk)) - 1, 0)):
        power_s = split_all([_dot3(power_s[h], power_s[h]) for h in heads])
        inv_s = split_all(inv)
        inv = [inv[h] + _dot3(inv_s[h], power_s[h]) for h in heads]
    x_s = split_all([x_state[h] + _dot3(a_ak[h], v_s[h]) for h in heads])
    inv_s = split_all(inv)
    u_s = split_all([_dot3(inv_s[h], x_s[h]) for h in heads])
    diag = (lax.broadcasted_iota(jnp.int32, (HEAD_DIM, HEAD_DIM), 0)
            == lax.broadcasted_iota(jnp.int32, (HEAD_DIM, HEAD_DIM), 1))
    if stacked:
        uv_s = [stack(u_s[h], v_s[h]) for h in heads]
        ys = [y_state[h] + _dot3(a_rbk[h], uv_s[h]) for h in heads]
        grown = [_dot3(stack(be_s[h], ke_s[h]), uv_s[h], TN_DIMS) for h in heads]
    else:
        ys = [y_state[h] + _dot3(a_rb[h], u_s[h]) + _dot3(a_rk[h], v_s[h]) for h in heads]
        grown = [_dot3(be_s[h], u_s[h], TN_DIMS) + _dot3(ke_s[h], v_s[h], TN_DIMS) for h in heads]
    for h in heads:
        pe = p_end[:, h * HEAD_DIM:(h + 1) * HEAD_DIM]
        pe_col = jnp.sum(jnp.where(diag, pe, 0.0), axis=1, keepdims=True)
        state_ref[h] = pe_col * s0[h] + grown[h]
    y = jnp.concatenate(ys, axis=1)

    inv_n = 1.0 / HEAD_DIM
    mean = _dot_exact_rhs(y, hsum) * inv_n
    yc = y - mean
    var = _dot_exact_rhs(yc * yc, hsum) * inv_n
    yn = yc * lax.rsqrt(var + GN_EPS) * lnw_ref[...] + lnb_ref[...]
    bonus = _dot_exact_rhs(r * k * rk_ref[...], hsum) * v
    y_ref[...] = ((yn + bonus) * g).astype(y_ref.dtype)

    @pl.when(c == pl.num_programs(1) - 1)
    def _():
        for h in range(HEADS):
            sout_ref[0, h] = state_ref[h].T


def _rwkv(p_all, row0, n_seq, t_len, chunk, shift0, wkv0, rw):
    mu, w0, decay_up, a0, aaa_up, gate_up, k_k, k_a, r_k, ln_w, ln_b = rw
    n_chunks = t_len // chunk
    blk0 = row0 // chunk
    row = lambda a: a.reshape(1, -1).astype(F32)
    hsum = jnp.asarray((np.arange(WIDTH)[:, None] // HEAD_DIM == np.arange(WIDTH)[None, :] // HEAD_DIM)
                       .astype(np.float32), dtype=BF16)
    consts = [row(mu), row(w0), decay_up, row(a0), aaa_up, gate_up, row(k_k), row(k_a),
              row(r_k), row(ln_w), row(ln_b), hsum]
    const = lambda a: pl.BlockSpec(a.shape, lambda b, c: (0,) * a.ndim)
    shift3 = shift0.reshape(n_seq, 1, RW_PROJ)
    return pl.pallas_call(
        functools.partial(_rwkv_kernel, chunk=chunk),
        grid=(n_seq, n_chunks),
        in_specs=[pl.BlockSpec((chunk, RW_PROJ), lambda b, c: (blk0 + b * n_chunks + c, 0)),
                  pl.BlockSpec((1, 1, RW_PROJ), lambda b, c: (b, 0, 0)),
                  pl.BlockSpec((1, HEADS, HEAD_DIM, HEAD_DIM), lambda b, c: (b, 0, 0, 0))]
                 + [const(a) for a in consts],
        out_specs=[pl.BlockSpec((chunk, WIDTH), lambda b, c: (b * n_chunks + c, 0)),
                   pl.BlockSpec((1, HEADS, HEAD_DIM, HEAD_DIM), lambda b, c: (b, 0, 0, 0))],
        out_shape=[jax.ShapeDtypeStruct((n_seq * t_len, WIDTH), BF16),
                   jax.ShapeDtypeStruct((n_seq, HEADS, HEAD_DIM, HEAD_DIM), F32)],
        scratch_shapes=[pltpu.VMEM((HEADS, HEAD_DIM, HEAD_DIM), F32), pltpu.VMEM((1, RW_PROJ), F32)],
        compiler_params=pltpu.CompilerParams(dimension_semantics=("arbitrary", "arbitrary"),
                                             vmem_limit_bytes=VMEM_LIMIT),
        name=f"rwkv_chunk{chunk}",
    )(p_all, shift3, wkv0, *consts)


def _merge_kernel(x_ref, attn_p_ref, attn_s_ref, rw_p_ref, rw_s_ref, gate_ref, wb0_ref, wb1_ref, wo_ref,
                  nf_ref, wr_ref, br_ref, x1_ref, h2_ref, idx_ref, top_ref, *, prompt_tiles):
    sample = pl.program_id(0) >= prompt_tiles
    attn = jnp.where(sample, attn_s_ref[...], attn_p_ref[...])
    rw = jnp.where(sample, rw_s_ref[...], rw_p_ref[...])
    gates = gate_ref[...]
    merged = (gates[:, :D_MODEL] * _dot(attn, wb0_ref[...])
              + gates[:, D_MODEL:] * _dot(rw, wb1_ref[...]))
    x1 = x_ref[...] + _dot(merged.astype(BF16), wo_ref[...])
    x1_ref[...] = x1
    h2 = _rms(x1, nf_ref[...])
    h2_ref[...] = h2
    logits = _dotf(h2, wr_ref[...]) + br_ref[...]
    lane = lax.broadcasted_iota(jnp.int32, logits.shape, 1)
    vals, idxs = [], []
    for _ in range(TOP_K):
        best = jnp.max(logits, axis=-1, keepdims=True)
        where = jnp.min(jnp.where(logits == best, lane, N_EXPERTS), axis=-1, keepdims=True)
        vals.append(best)
        idxs.append(where)
        logits = jnp.where(lane == where, -jnp.inf, logits)
    top = jnp.concatenate(vals, axis=1)
    e = jnp.exp(top - top[:, 0:1])
    top_ref[...] = e / jnp.sum(e, axis=-1, keepdims=True)
    idx_ref[...] = jnp.concatenate(idxs, axis=1)


def _merge_and_route(x, attn_p, attn_s, rw_p, rw_s, gates, w_branch, w_out, norm_ffn, w_router, b_router):
    rows = x.shape[0]
    tm = ROW_TILE
    prompt_tiles = attn_p.shape[0] // tm
    prompt_spec = pl.BlockSpec((tm, WIDTH), lambda i: (jnp.minimum(i, prompt_tiles - 1), 0))
    sample_spec = pl.BlockSpec((tm, WIDTH), lambda i: (jnp.maximum(i - prompt_tiles, 0), 0))
    wb0 = w_branch[0].astype(BF16)
    wb1 = w_branch[1].astype(BF16)
    wo = w_out.astype(BF16)
    nf = norm_ffn.reshape(1, D_MODEL)
    br = b_router.reshape(1, N_EXPERTS).astype(F32)
    wr = w_router.astype(F32)
    row_spec = lambda n: pl.BlockSpec((tm, n), lambda i: (i, 0))
    full = lambda a: pl.BlockSpec(a.shape, lambda i: (0,) * a.ndim)
    return pl.pallas_call(
        functools.partial(_merge_kernel, prompt_tiles=prompt_tiles),
        grid=(rows // tm,),
        in_specs=[row_spec(D_MODEL), prompt_spec, sample_spec, prompt_spec, sample_spec, row_spec(2 * D_MODEL),
                  full(wb0), full(wb1), full(wo), full(nf), full(wr), full(br)],
        out_specs=[row_spec(D_MODEL), row_spec(D_MODEL), row_spec(TOP_K), row_spec(TOP_K)],
        out_shape=[jax.ShapeDtypeStruct((rows, D_MODEL), F32),
                   jax.ShapeDtypeStruct((rows, D_MODEL), F32),
                   jax.ShapeDtypeStruct((rows, TOP_K), jnp.int32),
                   jax.ShapeDtypeStruct((rows, TOP_K), F32)],
        compiler_params=pltpu.CompilerParams(dimension_semantics=("arbitrary",),
                                             vmem_limit_bytes=VMEM_LIMIT),
        name="merge_route",
    )(x, attn_p, attn_s, rw_p, rw_s, gates, wb0, wb1, wo, nf, wr, br)


def _expert_kernel(be_ref, first_ref, used_ref, hs_ref, w1_ref, b1_ref, w2_ref, b2_ref, ys_ref,
                   w1b_ref, w2b_ref):
    i = pl.program_id(0)
    d_expert = w2_ref.shape[1]

    @pl.when(first_ref[i] == 1)
    def _():
        w1b_ref[...] = w1_ref[0].astype(BF16)
        w2b_ref[...] = w2_ref[0].astype(BF16)

    @pl.when(i < used_ref[0])
    def _():
        gu = _dot(hs_ref[...].astype(BF16), w1b_ref[...]) + b1_ref[0]
        gt = jnp.minimum(gu[:, :d_expert], SWIGLU_LIMIT)
        up = jnp.clip(gu[:, d_expert:], -SWIGLU_LIMIT, SWIGLU_LIMIT)
        act = (up + 1.0) * (gt * jax.nn.sigmoid(SWIGLU_ALPHA * gt))
        ys_ref[...] = _dot(act.astype(BF16), w2b_ref[...]) + b2_ref[0]

    @pl.when(i >= used_ref[0])
    def _():
        ys_ref[...] = jnp.zeros_like(ys_ref)


def _experts(hs, block_e, first, used, w1, b1, w2, b2):
    tmb = MOE_TILE
    n_blocks = hs.shape[0] // tmb
    d_expert = w2.shape[1]
    b1r = b1.reshape(N_EXPERTS, 1, 2 * d_expert)
    b2r = b2.reshape(N_EXPERTS, 1, D_MODEL)
    grid_spec = pltpu.PrefetchScalarGridSpec(
        num_scalar_prefetch=3,
        grid=(n_blocks,),
        in_specs=[pl.BlockSpec((tmb, D_MODEL), lambda i, be, fi, us: (i, 0)),
                  pl.BlockSpec((1, D_MODEL, 2 * d_expert), lambda i, be, fi, us: (be[i], 0, 0)),
                  pl.BlockSpec((1, 1, 2 * d_expert), lambda i, be, fi, us: (be[i], 0, 0)),
                  pl.BlockSpec((1, d_expert, D_MODEL), lambda i, be, fi, us: (be[i], 0, 0)),
                  pl.BlockSpec((1, 1, D_MODEL), lambda i, be, fi, us: (be[i], 0, 0))],
        out_specs=pl.BlockSpec((tmb, D_MODEL), lambda i, be, fi, us: (i, 0)),
        scratch_shapes=[pltpu.VMEM((D_MODEL, 2 * d_expert), BF16), pltpu.VMEM((d_expert, D_MODEL), BF16)],
    )
    return pl.pallas_call(
        _expert_kernel,
        grid_spec=grid_spec,
        out_shape=jax.ShapeDtypeStruct((n_blocks * tmb, D_MODEL), F32),
        compiler_params=pltpu.CompilerParams(dimension_semantics=("arbitrary",),
                                             vmem_limit_bytes=VMEM_LIMIT),
        name="experts",
    )(block_e, first, used, hs, w1, b1r, w2, b2r)


def _route_tables(top_idx, n_rows):
    tmb = MOE_TILE
    nk = n_rows * TOP_K
    n_blocks = nk // tmb + N_EXPERTS
    experts = jnp.arange(N_EXPERTS, dtype=jnp.int32)
    onehot = top_idx[:, :, None] == experts
    member = jnp.any(onehot, axis=1).astype(jnp.int32)
    counts = jnp.sum(member, axis=0)
    rank = jnp.cumsum(member, axis=0) - member
    padded = (counts + tmb - 1) // tmb * tmb
    pad_end = jnp.cumsum(padded)
    pad_start = pad_end - padded
    start = jnp.cumsum(counts) - counts
    slot_pos = jnp.sum(jnp.where(onehot, (pad_start + rank)[:, None, :], 0), axis=-1).astype(jnp.int32)
    order = jnp.argsort(top_idx.reshape(-1), stable=True).astype(jnp.int32)
    block_start = jnp.arange(n_blocks, dtype=jnp.int32) * tmb
    block_e = jnp.minimum(jnp.sum((pad_end[None, :] <= block_start[:, None]).astype(jnp.int32), axis=1),
                          N_EXPERTS - 1)
    row_e = jnp.repeat(block_e, tmb)
    in_group = jnp.arange(n_blocks * tmb, dtype=jnp.int32) - pad_start[row_e]
    valid = in_group < counts[row_e]
    sorted_slot = jnp.clip(start[row_e] + in_group, 0, nk - 1)
    row_tok = jnp.where(valid, order[sorted_slot] // TOP_K, 0).astype(jnp.int32)
    first = jnp.concatenate([jnp.ones((1,), jnp.int32),
                             (block_e[1:] != block_e[:-1]).astype(jnp.int32)])
    used = (pad_end[-1] // tmb).astype(jnp.int32).reshape(1)
    return row_tok, slot_pos, block_e.astype(jnp.int32), first, used


SC_CORES = 2
SC_SUBCORES = 16
SC_VMEM_ROWS = 112


def _gather_rows(table, idx):
    n, d = idx.shape[0], table.shape[1]
    workers = SC_CORES * SC_SUBCORES
    per_worker = n // workers
    assert n % workers == 0
    chunk = max(c for c in range(8, SC_VMEM_ROWS // 2 + 1, 8) if per_worker % (2 * c) == 0)
    mesh = plsc.VectorSubcoreMesh(core_axis_name="core", subcore_axis_name="subcore",
                                  num_cores=SC_CORES, num_subcores=SC_SUBCORES)
    buffer_types = [pltpu.VMEM((chunk,), jnp.int32), pltpu.VMEM((chunk, d), table.dtype),
                    pltpu.SemaphoreType.DMA, pltpu.SemaphoreType.DMA]

    @functools.partial(pl.kernel, mesh=mesh, out_type=jax.ShapeDtypeStruct((n, d), table.dtype),
                       scratch_types=buffer_types + buffer_types)
    def gather(table_hbm, idx_hbm, out_hbm, idx_a, rows_a, gsem_a, wsem_a, idx_b, rows_b, gsem_b, wsem_b):
        worker = lax.axis_index("subcore") * SC_CORES + lax.axis_index("core")

        def start_gather(c, idx_vmem, rows_vmem, sem):
            base = pl.multiple_of(worker * per_worker + c * chunk, 8)
            pltpu.sync_copy(idx_hbm.at[pl.ds(base, chunk)], idx_vmem)
            return pltpu.async_copy(table_hbm.at[idx_vmem], rows_vmem, sem)

        def start_write(c, rows_vmem, sem):
            base = pl.multiple_of(worker * per_worker + c * chunk, 8)
            return pltpu.async_copy(rows_vmem, out_hbm.at[pl.ds(base, chunk)], sem)

        @pl.loop(0, per_worker // (2 * chunk))
        def _(pair):
            gather_a = start_gather(2 * pair, idx_a, rows_a, gsem_a)
            gather_b = start_gather(2 * pair + 1, idx_b, rows_b, gsem_b)
            gather_a.wait()
            write_a = start_write(2 * pair, rows_a, wsem_a)
            gather_b.wait()
            write_b = start_write(2 * pair + 1, rows_b, wsem_b)
            write_a.wait()
            write_b.wait()

    return gather(table, idx)


def _combine_kernel(x1_ref, ys_ref, gate_ref, nf_ref, yp_ref, ys_out_ref, *, prompt_tiles):
    acc = x1_ref[...]
    gate = gate_ref[...]
    for j in range(TOP_K):
        acc = acc + gate[:, j:j + 1] * ys_ref[j]
    y = _rms(acc, nf_ref[...])
    i = pl.program_id(0)

    @pl.when(i < prompt_tiles)
    def _():
        yp_ref[0] = y

    @pl.when(i >= prompt_tiles)
    def _():
        ys_out_ref[...] = y


def _combine(x1, ys_slots, gate, norm_final, batch, t_pad, seq):
    rows = x1.shape[0]
    tm = ROW_TILE
    assert t_pad - seq == tm, "one leading tile of padding + meta rows per sequence"
    per_seq = t_pad // tm
    prompt_tiles = batch * per_seq
    n_sample = rows - batch * t_pad
    nf = norm_final.reshape(1, D_MODEL)

    def prompt_map(i):
        j = jnp.minimum(i, prompt_tiles - 1)
        return (j // per_seq, jnp.maximum(j % per_seq - 1, 0), 0)

    return pl.pallas_call(
        functools.partial(_combine_kernel, prompt_tiles=prompt_tiles),
        grid=(rows // tm,),
        in_specs=[pl.BlockSpec((tm, D_MODEL), lambda i: (i, 0)),
                  pl.BlockSpec((TOP_K, tm, D_MODEL), lambda i: (0, i, 0)),
                  pl.BlockSpec((tm, TOP_K), lambda i: (i, 0)),
                  pl.BlockSpec((1, D_MODEL), lambda i: (0, 0))],
        out_specs=[pl.BlockSpec((1, tm, D_MODEL), prompt_map),
                   pl.BlockSpec((tm, D_MODEL), lambda i: (jnp.maximum(i - prompt_tiles, 0), 0))],
        out_shape=[jax.ShapeDtypeStruct((batch, seq, D_MODEL), F32),
                   jax.ShapeDtypeStruct((n_sample, D_MODEL), F32)],
        compiler_params=pltpu.CompilerParams(dimension_semantics=("arbitrary",),
                                             vmem_limit_bytes=VMEM_LIMIT),
        name="combine_norm",
    )(x1, ys_slots, gate, nf)


def kernel(x_prompt, x_sample, cache_k, cache_v, state_wkv, state_shift, page_table, meta, norm_mix, w_in, sb_bias, rw_mu, rw_w0, rw_decay_up, rw_a0, rw_aaa_up, rw_gate_up, rw_k_k, rw_k_a, rw_r_k, rw_ln_w, rw_ln_b, w_gate, b_gate, w_branch, w_out, norm_ffn, w_router, b_router, w1, b1, w2, b2, norm_final):
    assert w_in.shape[0] == 1, "single layer"
    batch, seq, _ = x_prompt.shape
    n_seq, n_new, _ = x_sample.shape
    t_real = N_META + seq
    t_pad = -(-t_real // ATT_TILE) * ATT_TILE
    pad = t_pad - t_real
    first_tok = pad + N_META
    assert t_pad % RWKV_CHUNK == 0 and (batch * t_pad) % ROW_TILE == 0 and (n_seq * n_new) % ROW_TILE == 0
    row_s = batch * t_pad

    x_all, qb, k_all, v_all, kb, vb, p_all, gates = _projections(
        x_prompt, x_sample.reshape(n_seq * n_new, D_MODEL), meta, t_pad, norm_mix[0], w_in[0], w_gate[0],
        b_gate[0])

    bias = sb_bias[0].astype(F32) * LOG2E
    attn_p = _prompt_attention(qb, kb, vb, bias, batch, t_pad, pad)
    attn_s = _paged_attention(qb, k_all, v_all, cache_k, cache_v, page_table, bias, row_s, n_seq, n_new)

    rw = (rw_mu[0], rw_w0[0], rw_decay_up[0], rw_a0[0], rw_aaa_up[0], rw_gate_up[0],
          rw_k_k[0], rw_k_a[0], rw_r_k[0], rw_ln_w[0], rw_ln_b[0])
    y_p, wkv_p = _rwkv(p_all, 0, batch, t_pad, RWKV_CHUNK,
                       jnp.zeros((batch, RW_PROJ), F32),
                       jnp.zeros((batch, HEADS, HEAD_DIM, HEAD_DIM), F32), rw)
    y_s, wkv_s = _rwkv(p_all, row_s, n_seq, n_new, n_new, state_shift[0], state_wkv[0], rw)

    x1, h2, top_idx, top_gate = _merge_and_route(x_all, attn_p, attn_s, y_p, y_s, gates, w_branch[0],
                                                 w_out[0], norm_ffn[0], w_router[0], b_router[0])

    n_rows = x_all.shape[0]
    row_tok, slot_pos, block_e, first, used = _route_tables(top_idx, n_rows)
    hs = _gather_rows(h2, row_tok)
    ys = _experts(hs, block_e, first, used, w1[0], b1[0], w2[0], b2[0])
    ys_slots = _gather_rows(ys, slot_pos.T.reshape(-1)).reshape(TOP_K, n_rows, D_MODEL)
    y_prompt, y_sample = _combine(x1, ys_slots, top_gate, norm_final, batch, t_pad, seq)
    y_sample = y_sample.reshape(n_seq, n_new, D_MODEL)

    def prompt_rows(a, lo):
        return jnp.stack([a[b * t_pad + lo:(b + 1) * t_pad] for b in range(batch)])

    k_p = prompt_rows(k_all, pad).reshape(1, batch, t_real, HEADS, HEAD_DIM)
    v_p = prompt_rows(v_all, pad).reshape(1, batch, t_real, HEADS, HEAD_DIM)
    shift_p = prompt_rows(p_all, t_pad - 1).reshape(1, batch, RW_PROJ)
    k_s = k_all[row_s:].reshape(1, n_seq, n_new, HEADS, HEAD_DIM)
    v_s = v_all[row_s:].reshape(1, n_seq, n_new, HEADS, HEAD_DIM)
    shift_s = p_all[row_s:].reshape(n_seq, n_new, RW_PROJ)[:, -1].reshape(1, n_seq, RW_PROJ)
    return (y_prompt, y_sample, k_p, v_p, wkv_p[None], shift_p, k_s, v_s, wkv_s[None], shift_s)
```

```python
import functools
import math

import numpy as np
import jax
import jax.numpy as jnp
from jax import lax
from jax.experimental import pallas as pl
from jax.experimental.pallas import tpu as pltpu
from jax.experimental.pallas import tpu_sc as plsc

F32 = jnp.float32
BF16 = jnp.bfloat16
HIGHEST = lax.Precision.HIGHEST

D_MODEL = 1024
N_META = 16
HEADS = 8
HEAD_DIM = 64
WIDTH = HEADS * HEAD_DIM
LOG2E = math.log2(math.e)
SB_SCALE2 = LOG2E / math.sqrt(HEAD_DIM)
DECAY_LORA = 64
AAA_LORA = 64
GATE_LORA = 128
RW_PROJ = 3 * WIDTH + DECAY_LORA + AAA_LORA + GATE_LORA
GN_EPS = 64e-5
N_EXPERTS = 32
TOP_K = 4
SWIGLU_LIMIT = 7.0
SWIGLU_ALPHA = 1.702
RMS_EPS = 1e-5

ROW_TILE = 256
MERGE_TILE = 512
ATT_TILE = 256
RWKV_CHUNK = 64
PAGES_PER_STEP = 16
MOE_TILE = 256
VMEM_LIMIT = 56 * 1024 * 1024

NT_DIMS = (((1,), (1,)), ((), ()))
TN_DIMS = (((0,), (0,)), ((), ()))
NN_DIMS = (((1,), (0,)), ((), ()))


def _dot(a, b):
    return jnp.dot(a, b, preferred_element_type=F32)


def _dotf(a, b, dims=NN_DIMS):
    return lax.dot_general(a, b, dims, precision=HIGHEST, preferred_element_type=F32)


def _split2(x):
    hi = x.astype(BF16)
    return hi, (x - hi.astype(F32)).astype(BF16)


def _dot3(a, b, dims=NN_DIMS):
    dg = lambda x, y: lax.dot_general(x, y, dims, preferred_element_type=F32)
    return dg(a[0], b[0]) + dg(a[0], b[1]) + dg(a[1], b[0])


def _split3(x):
    hi = x.astype(BF16)
    r1 = x - hi.astype(F32)
    mid = r1.astype(BF16)
    lo = (r1 - mid.astype(F32)).astype(BF16)
    return hi, mid, lo


def _dot_exact_rhs(x, m):
    hi, mid, lo = _split3(x)
    return _dot(hi, m) + _dot(mid, m) + _dot(lo, m)


def _dot_exact_lhs(m, x):
    hi, mid, lo = _split3(x)
    return _dot(m, hi) + _dot(m, mid) + _dot(m, lo)


def _neg_softplus(z):
    return jnp.minimum(-z, 0.0) - jnp.log1p(jnp.exp(-jnp.abs(z)))


def _rms(x, g):
    return x * lax.rsqrt(jnp.mean(x * x, axis=-1, keepdims=True) + RMS_EPS) * g


def _proj_kernel(xp_ref, xs_ref, meta_ref, g_ref, wqkv_ref, wp_ref, wg_ref, bg_ref,
                 x_ref, q_ref, k_ref, v_ref, kb_ref, vb_ref, p_ref, gate_ref, *, per_seq, prompt_tiles):
    i = pl.program_id(0)
    tm = x_ref.shape[0]
    leading = jnp.logical_and(i < prompt_tiles, lax.rem(i, per_seq) == 0)

    @pl.when(leading)
    def _():
        x_ref[0:tm - N_META, :] = jnp.zeros((tm - N_META, D_MODEL), F32)
        x_ref[tm - N_META:tm, :] = meta_ref[...]

    @pl.when(jnp.logical_and(i < prompt_tiles, jnp.logical_not(leading)))
    def _():
        x_ref[...] = xp_ref[0]

    @pl.when(i >= prompt_tiles)
    def _():
        x_ref[...] = xs_ref[...]

    hb = _rms(x_ref[...], g_ref[...]).astype(BF16)
    qkv = _dot(hb, wqkv_ref[...])
    k = qkv[:, WIDTH:2 * WIDTH]
    v = qkv[:, 2 * WIDTH:]
    q_ref[...] = (qkv[:, :WIDTH] * SB_SCALE2).astype(BF16)
    k_ref[...] = k
    v_ref[...] = v
    kb_ref[...] = k.astype(BF16)
    vb_ref[...] = v.astype(BF16)
    p_ref[...] = _dot(hb, wp_ref[...])
    gate_ref[...] = jax.nn.sigmoid(_dot(hb, wg_ref[...]) + bg_ref[...])


def _projections(x_prompt, x_sample, meta, t_pad, norm_g, w_in, w_gate, b_gate):
    batch, seq, _ = x_prompt.shape
    tm = ROW_TILE
    assert t_pad - seq == tm and x_sample.shape[0] % tm == 0
    per_seq = t_pad // tm
    prompt_tiles = batch * per_seq
    rows = batch * t_pad + x_sample.shape[0]
    wqkv = w_in[:, :3 * WIDTH].astype(BF16)
    wp = w_in[:, 3 * WIDTH:].astype(BF16)
    wg = w_gate.astype(BF16)
    row_spec = lambda n: pl.BlockSpec((tm, n), lambda i: (i, 0))
    full = lambda a: pl.BlockSpec(a.shape, lambda i: (0,) * a.ndim)
    g2 = norm_g.reshape(1, D_MODEL)
    bg = b_gate.reshape(1, 2 * D_MODEL)

    def prompt_map(i):
        j = jnp.minimum(i, prompt_tiles - 1)
        return (j // per_seq, jnp.maximum(j % per_seq - 1, 0), 0)

    return pl.pallas_call(
        functools.partial(_proj_kernel, per_seq=per_seq, prompt_tiles=prompt_tiles),
        grid=(rows // tm,),
        in_specs=[pl.BlockSpec((1, tm, D_MODEL), prompt_map),
                  pl.BlockSpec((tm, D_MODEL), lambda i: (jnp.maximum(i - prompt_tiles, 0), 0)),
                  full(meta), full(g2), full(wqkv), full(wp), full(wg), full(bg)],
        out_specs=[row_spec(D_MODEL)] + [row_spec(WIDTH)] * 5 + [row_spec(RW_PROJ), row_spec(2 * D_MODEL)],
        out_shape=[jax.ShapeDtypeStruct((rows, D_MODEL), F32),
                   jax.ShapeDtypeStruct((rows, WIDTH), BF16),
                   jax.ShapeDtypeStruct((rows, WIDTH), F32),
                   jax.ShapeDtypeStruct((rows, WIDTH), F32),
                   jax.ShapeDtypeStruct((rows, WIDTH), BF16),
                   jax.ShapeDtypeStruct((rows, WIDTH), BF16),
                   jax.ShapeDtypeStruct((rows, RW_PROJ), F32),
                   jax.ShapeDtypeStruct((rows, 2 * D_MODEL), F32)],
        compiler_params=pltpu.CompilerParams(dimension_semantics=("arbitrary",),
                                             vmem_limit_bytes=VMEM_LIMIT),
        name="projections",
    )(x_prompt, x_sample, meta.astype(F32), g2, wqkv, wp, wg, bg)


def _stick_logs(z, visible):
    neg_abs = lax.bitcast_convert_type(
        lax.bitcast_convert_type(z, jnp.uint32) | jnp.uint32(0x80000000), F32)
    sp = jnp.maximum(z, 0.0) + jnp.log2(1.0 + jnp.exp2(neg_abs))
    log_beta = z - sp
    if visible is not None:
        sp = jnp.where(visible, sp, 0.0)
    hi = sp.astype(BF16)
    lo = (sp - hi.astype(F32)).astype(BF16)
    return jnp.concatenate([hi, lo], axis=1), log_beta, jnp.sum(sp, axis=-1, keepdims=True)


def _stick_weights(log_beta, after, visible):
    w = jnp.exp2(log_beta - after)
    if visible is not None:
        w = jnp.where(visible, w, 0.0)
    return w


def _suffix_matrix(n):
    j = np.arange(n)
    m = (j[:, None] > j[None, :]).astype(np.float32)
    return jnp.asarray(np.concatenate([m, m], axis=0), dtype=BF16)


def _prompt_attn_kernel(qi_ref, kt_ref, bias_ref, q_ref, k1_ref, v1_ref, k0_ref, v0_ref, m_ref, o_ref,
                        acc_ref, carry_ref, *, tile, pad):
    t = pl.program_id(1)
    i = qi_ref[t]
    j0 = kt_ref[t]
    first = j0 + 1 >= i

    @pl.when(first)
    def _():
        acc_ref[...] = jnp.zeros_like(acc_ref)
        carry_ref[...] = jnp.zeros_like(carry_ref)

    t_far = pl.num_programs(1) * 2 * tile
    subs = ((j0 + 1, k1_ref, v1_ref), (j0, k0_ref, v0_ref))

    def process(masked):
        q, m2 = q_ref[...], m_ref[...]
        head = lambda h: slice(h * HEAD_DIM, (h + 1) * HEAD_DIM)
        items = []
        for j, k_ref, v_ref in subs:
            visible = None
            if masked:
                q_pos = i * tile + lax.broadcasted_iota(jnp.int32, (tile, tile), 0)
                k_pos = j * tile + lax.broadcasted_iota(jnp.int32, (tile, tile), 1)
                visible = jnp.where(k_pos >= pad, k_pos, t_far) < q_pos
            k, v = k_ref[...], v_ref[...]
            items += [(h, k, v, visible) for h in range(HEADS)]
        scores, logs, afters = {}, {}, {}
        outs = [None] * HEADS
        for r in range(len(items) + 2):
            if r < len(items):
                h, k, _, _ = items[r]
                scores[r] = lax.dot_general(q[:, head(h)], k[:, head(h)], NT_DIMS,
                                            preferred_element_type=F32)
            if 0 <= r - 1 < len(items):
                h, _, _, visible = items[r - 1]
                hilo, log_beta, total = _stick_logs(scores.pop(r - 1) + bias_ref[h], visible)
                afters[r - 1] = _dot(hilo, m2)
                logs[r - 1] = (log_beta, total)
            if 0 <= r - 2 < len(items):
                h, _, v, visible = items[r - 2]
                log_beta, total = logs.pop(r - 2)
                w = _stick_weights(log_beta, afters.pop(r - 2), visible)
                carry = carry_ref[h]
                out = jnp.exp2(-carry) * _dot(w.astype(BF16), v[:, head(h)])
                outs[h] = out if outs[h] is None else outs[h] + out
                carry_ref[h] = carry + total
        acc_ref[...] += jnp.concatenate(outs, axis=1)

    needs_mask = jnp.logical_or(first, j0 == 0)
    pl.when(needs_mask)(lambda: process(True))
    pl.when(jnp.logical_not(needs_mask))(lambda: process(False))

    @pl.when(j0 == 0)
    def _():
        o_ref[...] = acc_ref[...].astype(o_ref.dtype)


def _prompt_attention(qb, kb, vb, bias2, batch, t_pad, pad):
    tile = ATT_TILE
    nq = t_pad // tile
    qi = np.concatenate([np.full(i // 2 + 1, i) for i in range(nq)]).astype(np.int32)
    kt = np.concatenate([2 * np.arange(i // 2, -1, -1) for i in range(nq)]).astype(np.int32)
    q_map = lambda b, t, qi, kt: (b * nq + qi[t], 0)
    hi_map = lambda b, t, qi, kt: (b * nq + jnp.minimum(kt[t] + 1, nq - 1), 0)
    lo_map = lambda b, t, qi, kt: (b * nq + kt[t], 0)
    grid_spec = pltpu.PrefetchScalarGridSpec(
        num_scalar_prefetch=2,
        grid=(batch, len(qi)),
        in_specs=[
            pl.BlockSpec(memory_space=pltpu.SMEM),
            pl.BlockSpec((tile, WIDTH), q_map),
            pl.BlockSpec((tile, WIDTH), hi_map),
            pl.BlockSpec((tile, WIDTH), hi_map),
            pl.BlockSpec((tile, WIDTH), lo_map),
            pl.BlockSpec((tile, WIDTH), lo_map),
            pl.BlockSpec((2 * tile, tile), lambda b, t, qi, kt: (0, 0)),
        ],
        out_specs=pl.BlockSpec((tile, WIDTH), q_map),
        scratch_shapes=[pltpu.VMEM((tile, WIDTH), F32), pltpu.VMEM((HEADS, tile, 1), F32)],
    )
    return pl.pallas_call(
        functools.partial(_prompt_attn_kernel, tile=tile, pad=pad),
        grid_spec=grid_spec,
        out_shape=jax.ShapeDtypeStruct((batch * t_pad, WIDTH), BF16),
        compiler_params=pltpu.CompilerParams(dimension_semantics=("arbitrary", "arbitrary"),
                                             vmem_limit_bytes=VMEM_LIMIT),
        name="prompt_attention",
    )(jnp.asarray(qi), jnp.asarray(kt), bias2, qb, kb, vb, kb, vb, _suffix_matrix(tile))


def _paged_attn_kernel(pt_ref, bias_ref, hmask_ref, m_ref, q_ref, kn_ref, vn_ref, *rest,
                       n_new, page, pages_per_step):
    page_refs = rest[:2 * pages_per_step]
    o_ref = rest[2 * pages_per_step]
    qbd_ref, acc_ref, carry_ref, knew_ref, vnew_ref = rest[2 * pages_per_step + 1:]
    s = pl.program_id(1)
    rows = HEADS * n_new

    def scores(keys, dim_major):
        dims = NN_DIMS if dim_major else NT_DIMS
        return (lax.dot_general(qbd_ref[...], keys.astype(BF16), dims, preferred_element_type=F32)
                + bias_ref[...])

    def logs(z, visible):
        hilo, log_beta, total = _stick_logs(z, visible)
        return _dot(hilo, m_ref[...]), log_beta, total

    def accumulate(after, log_beta, total, values, dim_major, visible):
        w = _stick_weights(log_beta, after, visible).astype(BF16)
        dims = NT_DIMS if dim_major else NN_DIMS
        out = lax.dot_general(w, values.astype(BF16), dims, preferred_element_type=F32)
        carry = carry_ref[...]
        acc_ref[...] += jnp.exp2(-carry) * out
        carry_ref[...] = carry + total

    @pl.when(s == 0)
    def _():
        q_rep = jnp.concatenate([q_ref[...]] * HEADS, axis=0)
        qbd_ref[...] = jnp.where(hmask_ref[...] > 0.0, q_rep, jnp.zeros_like(q_rep))
        acc_ref[...] = jnp.zeros_like(acc_ref)
        carry_ref[...] = jnp.zeros_like(carry_ref)
        knew_ref[...] = jnp.zeros_like(knew_ref)
        vnew_ref[...] = jnp.zeros_like(vnew_ref)
        knew_ref[0:n_new, :] = kn_ref[...]
        vnew_ref[0:n_new, :] = vn_ref[...]
        t_idx = lax.rem(lax.broadcasted_iota(jnp.int32, (rows, page), 0), n_new)
        c_idx = lax.broadcasted_iota(jnp.int32, (rows, page), 1)
        visible = c_idx < t_idx
        after, log_beta, total = logs(scores(knew_ref[...], False), visible)
        accumulate(after, log_beta, total, vnew_ref[...], False, visible)

    def channels(ref):
        return ref[0, 0].reshape(WIDTH, page)

    n = pages_per_step
    z, lg = {}, {}
    for r in range(n + 2):
        if r < n:
            z[r] = scores(channels(page_refs[2 * r]), True)
        if 0 <= r - 1 < n:
            lg[r - 1] = logs(z.pop(r - 1), None)
        if 0 <= r - 2 < n:
            accumulate(*lg.pop(r - 2), channels(page_refs[2 * (r - 2) + 1]), True, None)

    @pl.when(s == pl.num_programs(1) - 1)
    def _():
        own = acc_ref[...] * hmask_ref[...]
        o_ref[...] = own.reshape(HEADS, n_new, WIDTH).sum(axis=0).astype(o_ref.dtype)


def _paged_attention(qb, k_new, v_new, k_pool, v_pool, page_table, bias, row0, n_seq, n_new):
    n_pages = page_table.shape[1]
    page = k_pool.shape[2]
    k_pool = jnp.transpose(k_pool, (0, 1, 3, 4, 2))
    v_pool = jnp.transpose(v_pool, (0, 1, 3, 4, 2))
    pps = math.gcd(PAGES_PER_STEP, n_pages)
    steps = n_pages // pps
    rows = HEADS * n_new
    blk0 = row0 // n_new
    bias_col = jnp.repeat(bias.astype(F32), n_new).reshape(rows, 1)
    hmask = jnp.asarray((np.arange(rows)[:, None] // n_new == np.arange(WIDTH)[None, :] // HEAD_DIM)
                        .astype(np.float32))
    const = lambda a: pl.BlockSpec(a.shape, lambda b, s, pt: (0,) * a.ndim)
    new_spec = pl.BlockSpec((n_new, WIDTH), lambda b, s, pt: (blk0 + b, 0))

    def page_spec(p):
        return pl.BlockSpec((1, 1, HEADS, HEAD_DIM, page),
                            lambda b, s, pt: (0, pt[b, n_pages - 1 - (s * pps + p)], 0, 0, 0))

    page_specs, page_args = [], []
    for p in range(pps):
        page_specs += [page_spec(p), page_spec(p)]
        page_args += [k_pool, v_pool]
    m = _suffix_matrix(page)
    grid_spec = pltpu.PrefetchScalarGridSpec(
        num_scalar_prefetch=1,
        grid=(n_seq, steps),
        in_specs=[const(bias_col), const(hmask), const(m), new_spec, new_spec, new_spec] + page_specs,
        out_specs=pl.BlockSpec((n_new, WIDTH), lambda b, s, pt: (b, 0)),
        scratch_shapes=[pltpu.VMEM((rows, WIDTH), BF16), pltpu.VMEM((rows, WIDTH), F32),
                        pltpu.VMEM((rows, 1), F32), pltpu.VMEM((page, WIDTH), F32),
                        pltpu.VMEM((page, WIDTH), F32)],
    )
    return pl.pallas_call(
        functools.partial(_paged_attn_kernel, n_new=n_new, page=page, pages_per_step=pps),
        grid_spec=grid_spec,
        out_shape=jax.ShapeDtypeStruct((n_seq * n_new, WIDTH), BF16),
        compiler_params=pltpu.CompilerParams(dimension_semantics=("arbitrary", "arbitrary"),
                                             vmem_limit_bytes=VMEM_LIMIT),
        name="paged_attention",
    )(page_table, bias_col, hmask, m, qb, k_new, v_new, *page_args)


def _rwkv_kernel(p_ref, shift_ref, s0_ref, mu_ref, w0_ref, dup_ref, a0_ref, aup_ref, gup_ref,
                 kkw_ref, kaw_ref, rk_ref, lnw_ref, lnb_ref, hsum_ref,
                 y_ref, sout_ref, state_ref, prev_ref, *, chunk):
    c = pl.program_id(1)

    @pl.when(c == 0)
    def _():
        for h in range(HEADS):
            state_ref[h] = s0_ref[0, h].T
        prev_ref[...] = shift_ref[0]

    p = p_ref[...]
    row = lax.broadcasted_iota(jnp.int32, p.shape, 0)
    prev = jnp.where(row == 0, prev_ref[...], pltpu.roll(p, 1, 0))
    prev_ref[...] = p[chunk - 1:chunk, :]
    xm = p + (prev - p) * mu_ref[...]

    c1, c2, c3 = WIDTH, 2 * WIDTH, 3 * WIDTH
    c4 = c3 + DECAY_LORA
    c5 = c4 + AAA_LORA
    r, k, v = xm[:, :c1], xm[:, c1:c2], xm[:, c2:c3]
    w_lo, a_lo, g_lo = xm[:, c3:c4], xm[:, c4:c5], xm[:, c5:]
    lora = lambda x, w_ref: _dot3(_split2(x), _split2(w_ref[...]))
    log_w = _neg_softplus(-(w0_ref[...] + lora(jnp.tanh(w_lo), dup_ref))) - 0.5
    log_decay = -jnp.exp(log_w)
    a = jax.nn.sigmoid(a0_ref[...] + lora(a_lo, aup_ref))
    g = lora(jax.nn.sigmoid(g_lo), gup_ref)

    hsum = hsum_ref[...]
    kk = k * kkw_ref[...]
    kk = kk / jnp.maximum(jnp.sqrt(_dot_exact_rhs(kk * kk, hsum)), 1e-12)
    k = k * (1.0 + (a - 1.0) * kaw_ref[...])

    ti = lax.broadcasted_iota(jnp.int32, (chunk, chunk), 0)
    tj = lax.broadcasted_iota(jnp.int32, (chunk, chunk), 1)
    incl = ti >= tj
    strict = ti > tj
    cum = _dot_exact_lhs(jnp.where(incl, 1.0, 0.0).astype(BF16), log_decay)
    p_incl = jnp.exp(cum)
    p_inv = jnp.exp(-cum)
    p_end = p_incl[chunk - 1:chunk, :]
    a_t = -kk * jnp.exp(cum - log_decay)
    r_t = r * p_incl
    b_t = kk * a * p_inv
    k_t = k * p_inv
    b_e = b_t * p_end
    k_e = k_t * p_end
    eye = ti == tj

    heads = range(HEADS)
    per_head = lambda pair: [(pair[0][:, h * HEAD_DIM:(h + 1) * HEAD_DIM],
                              pair[1][:, h * HEAD_DIM:(h + 1) * HEAD_DIM]) for h in heads]
    split_all = lambda xs: [_split2(x) for x in xs]
    a_s, r_s, b_s, k_s, v_s = (per_head(_split2(x)) for x in (a_t, r_t, b_t, k_t, v))
    be_s, ke_s = per_head(_split2(b_e)), per_head(_split2(k_e))
    s0 = [state_ref[h] for h in heads]
    s0_s = split_all(s0)
    stack = lambda p, q: (jnp.concatenate([p[0], q[0]], axis=0), jnp.concatenate([p[1], q[1]], axis=0))
    stacked = chunk % 16 == 0
    if stacked:
        ar_s = [stack(a_s[h], r_s[h]) for h in heads]
        quad = [_dot3(ar_s[h], stack(b_s[h], k_s[h]), NT_DIMS) for h in heads]
        n_ab = [jnp.where(strict, quad[h][:chunk, :chunk], 0.0) for h in heads]
        a_ak = split_all([jnp.where(strict, quad[h][:chunk, chunk:], 0.0) for h in heads])
        col2 = lax.broadcasted_iota(jnp.int32, (chunk, 2 * chunk), 1)
        incl2 = (lax.broadcasted_iota(jnp.int32, (chunk, 2 * chunk), 0)
                 >= jnp.where(col2 >= chunk, col2 - chunk, col2))
        a_rbk = split_all([jnp.where(incl2, quad[h][chunk:, :], 0.0) for h in heads])
        on_state = [_dot3(ar_s[h], s0_s[h]) for h in heads]
        x_state = [t[:chunk] for t in on_state]
        y_state = [t[chunk:] for t in on_state]
    else:
        n_ab = [jnp.where(strict, _dot3(a_s[h], b_s[h], NT_DIMS), 0.0) for h in heads]
        a_ak = split_all([jnp.where(strict, _dot3(a_s[h], k_s[h], NT_DIMS), 0.0) for h in heads])
        a_rb = split_all([jnp.where(incl, _dot3(r_s[h], b_s[h], NT_DIMS), 0.0) for h in heads])
        a_rk = split_all([jnp.where(incl, _dot3(r_s[h], k_s[h], NT_DIMS), 0.0) for h in heads])
        x_state = [_dot3(a_s[h], s0_s[h]) for h in heads]
        y_state = [_dot3(r_s[h], s0_s[h]) for h in heads]
    inv = [jnp.where(eye, 1.0, 0.0) + n for n in n_ab]
    power_s = split_all(n_ab)
    for _ in range(max(int(math.log2(chunk)) - 1, 0)):
        power_s = split_all([_dot3(power_s[h], power_s[h]) for h in heads])
        inv_s = split_all(inv)
        inv = [inv[h] + _dot3(inv_s[h], power_s[h]) for h in heads]
    x_s = split_all([x_state[h] + _dot3(a_ak[h], v_s[h]) for h in heads])
    inv_s = split_all(inv)
    u_s = split_all([_dot3(inv_s[h], x_s[h]) for h in heads])
    diag = (lax.broadcasted_iota(jnp.int32, (HEAD_DIM, HEAD_DIM), 0)
            == lax.broadcasted_iota(jnp.int32, (HEAD_DIM, HEAD_DIM), 1))
    if stacked:
        uv_s = [stack(u_s[h], v_s[h]) for h in heads]
        ys = [y_state[h] + _dot3(a_rbk[h], uv_s[h]) for h in heads]
        grown = [_dot3(stack(be_s[h], ke_s[h]), uv_s[h], TN_DIMS) for h in heads]
    else:
        ys = [y_state[h] + _dot3(a_rb[h], u_s[h]) + _dot3(a_rk[h], v_s[h]) for h in heads]
        grown = [_dot3(be_s[h], u_s[h], TN_DIMS) + _dot3(ke_s[h], v_s[h], TN_DIMS) for h in heads]
    for h in heads:
        pe = p_end[:, h * HEAD_DIM:(h + 1) * HEAD_DIM]
        pe_col = jnp.sum(jnp.where(diag, pe, 0.0), axis=1, keepdims=True)
        state_ref[h] = pe_col * s0[h] + grown[h]
    y = jnp.concatenate(ys, axis=1)

    inv_n = 1.0 / HEAD_DIM
    mean = _dot_exact_rhs(y, hsum) * inv_n
    yc = y - mean
    var = _dot_exact_rhs(yc * yc, hsum) * inv_n
    yn = yc * lax.rsqrt(var + GN_EPS) * lnw_ref[...] + lnb_ref[...]
    bonus = _dot_exact_rhs(r * k * rk_ref[...], hsum) * v
    y_ref[...] = ((yn + bonus) * g).astype(y_ref.dtype)

    @pl.when(c == pl.num_programs(1) - 1)
    def _():
        for h in range(HEADS):
            sout_ref[0, h] = state_ref[h].T


def _rwkv(p_all, row0, n_seq, t_len, chunk, shift0, wkv0, rw):
    mu, w0, decay_up, a0, aaa_up, gate_up, k_k, k_a, r_k, ln_w, ln_b = rw
    n_chunks = t_len // chunk
    blk0 = row0 // chunk
    row = lambda a: a.reshape(1, -1).astype(F32)
    hsum = jnp.asarray((np.arange(WIDTH)[:, None] // HEAD_DIM == np.arange(WIDTH)[None, :] // HEAD_DIM)
                       .astype(np.float32), dtype=BF16)
    consts = [row(mu), row(w0), decay_up, row(a0), aaa_up, gate_up, row(k_k), row(k_a),
              row(r_k), row(ln_w), row(ln_b), hsum]
    const = lambda a: pl.BlockSpec(a.shape, lambda b, c: (0,) * a.ndim)
    shift3 = shift0.reshape(n_seq, 1, RW_PROJ)
    return pl.pallas_call(
        functools.partial(_rwkv_kernel, chunk=chunk),
        grid=(n_seq, n_chunks),
        in_specs=[pl.BlockSpec((chunk, RW_PROJ), lambda b, c: (blk0 + b * n_chunks + c, 0)),
                  pl.BlockSpec((1, 1, RW_PROJ), lambda b, c: (b, 0, 0)),
                  pl.BlockSpec((1, HEADS, HEAD_DIM, HEAD_DIM), lambda b, c: (b, 0, 0, 0))]
                 + [const(a) for a in consts],
        out_specs=[pl.BlockSpec((chunk, WIDTH), lambda b, c: (b * n_chunks + c, 0)),
                   pl.BlockSpec((1, HEADS, HEAD_DIM, HEAD_DIM), lambda b, c: (b, 0, 0, 0))],
        out_shape=[jax.ShapeDtypeStruct((n_seq * t_len, WIDTH), BF16),
                   jax.ShapeDtypeStruct((n_seq, HEADS, HEAD_DIM, HEAD_DIM), F32)],
        scratch_shapes=[pltpu.VMEM((HEADS, HEAD_DIM, HEAD_DIM), F32), pltpu.VMEM((1, RW_PROJ), F32)],
        compiler_params=pltpu.CompilerParams(dimension_semantics=("arbitrary", "arbitrary"),
                                             vmem_limit_bytes=VMEM_LIMIT),
        name=f"rwkv_chunk{chunk}",
    )(p_all, shift3, wkv0, *consts)


def _merge_kernel(x_ref, attn_p_ref, attn_s_ref, rw_p_ref, rw_s_ref, gate_ref, wb0_ref, wb1_ref, wo_ref,
                  nf_ref, wr_ref, br_ref, x1_ref, h2_ref, idx_ref, top_ref, *, prompt_tiles):
    sample = pl.program_id(0) >= prompt_tiles
    attn = jnp.where(sample, attn_s_ref[...], attn_p_ref[...])
    rw = jnp.where(sample, rw_s_ref[...], rw_p_ref[...])
    gates = gate_ref[...]
    merged = (gates[:, :D_MODEL] * _dot(attn, wb0_ref[...])
              + gates[:, D_MODEL:] * _dot(rw, wb1_ref[...]))
    x1 = x_ref[...] + _dot(merged.astype(BF16), wo_ref[...])
    x1_ref[...] = x1
    h2 = _rms(x1, nf_ref[...])
    h2_ref[...] = h2
    logits = _dotf(h2, wr_ref[...]) + br_ref[...]
    lane = lax.broadcasted_iota(jnp.int32, logits.shape, 1)
    vals, idxs = [], []
    for _ in range(TOP_K):
        best = jnp.max(logits, axis=-1, keepdims=True)
        where = jnp.min(jnp.where(logits == best, lane, N_EXPERTS), axis=-1, keepdims=True)
        vals.append(best)
        idxs.append(where)
        logits = jnp.where(lane == where, -jnp.inf, logits)
    top = jnp.concatenate(vals, axis=1)
    e = jnp.exp(top - top[:, 0:1])
    top_ref[...] = e / jnp.sum(e, axis=-1, keepdims=True)
    idx_ref[...] = jnp.concatenate(idxs, axis=1)


def _merge_and_route(x, attn_p, attn_s, rw_p, rw_s, gates, w_branch, w_out, norm_ffn, w_router, b_router):
    rows = x.shape[0]
    tm = MERGE_TILE if attn_p.shape[0] % MERGE_TILE == 0 and attn_s.shape[0] % MERGE_TILE == 0 else ROW_TILE
    prompt_tiles = attn_p.shape[0] // tm
    prompt_spec = pl.BlockSpec((tm, WIDTH), lambda i: (jnp.minimum(i, prompt_tiles - 1), 0))
    sample_spec = pl.BlockSpec((tm, WIDTH), lambda i: (jnp.maximum(i - prompt_tiles, 0), 0))
    wb0 = w_branch[0].astype(BF16)
    wb1 = w_branch[1].astype(BF16)
    wo = w_out.astype(BF16)
    nf = norm_ffn.reshape(1, D_MODEL)
    br = b_router.reshape(1, N_EXPERTS).astype(F32)
    wr = w_router.astype(F32)
    row_spec = lambda n: pl.BlockSpec((tm, n), lambda i: (i, 0))
    full = lambda a: pl.BlockSpec(a.shape, lambda i: (0,) * a.ndim)
    return pl.pallas_call(
        functools.partial(_merge_kernel, prompt_tiles=prompt_tiles),
        grid=(rows // tm,),
        in_specs=[row_spec(D_MODEL), prompt_spec, sample_spec, prompt_spec, sample_spec, row_spec(2 * D_MODEL),
                  full(wb0), full(wb1), full(wo), full(nf), full(wr), full(br)],
        out_specs=[row_spec(D_MODEL), row_spec(D_MODEL), row_spec(TOP_K), row_spec(TOP_K)],
        out_shape=[jax.ShapeDtypeStruct((rows, D_MODEL), F32),
                   jax.ShapeDtypeStruct((rows, D_MODEL), F32),
                   jax.ShapeDtypeStruct((rows, TOP_K), jnp.int32),
                   jax.ShapeDtypeStruct((rows, TOP_K), F32)],
        compiler_params=pltpu.CompilerParams(dimension_semantics=("arbitrary",),
                                             vmem_limit_bytes=VMEM_LIMIT),
        name="merge_route",
    )(x, attn_p, attn_s, rw_p, rw_s, gates, wb0, wb1, wo, nf, wr, br)


def _expert_kernel(be_ref, first_ref, used_ref, hs_ref, w1_ref, b1_ref, w2_ref, b2_ref, ys_ref,
                   w1b_ref, w2b_ref):
    i = pl.program_id(0)
    d_expert = w2_ref.shape[1]

    @pl.when(first_ref[i] == 1)
    def _():
        w1b_ref[...] = w1_ref[0].astype(BF16)
        w2b_ref[...] = w2_ref[0].astype(BF16)

    @pl.when(i < used_ref[0])
    def _():
        gu = _dot(hs_ref[...].astype(BF16), w1b_ref[...]) + b1_ref[0]
        gt = jnp.minimum(gu[:, :d_expert], SWIGLU_LIMIT)
        up = jnp.clip(gu[:, d_expert:], -SWIGLU_LIMIT, SWIGLU_LIMIT)
        act = (up + 1.0) * (gt * jax.nn.sigmoid(SWIGLU_ALPHA * gt))
        ys_ref[...] = _dot(act.astype(BF16), w2b_ref[...]) + b2_ref[0]

    @pl.when(i >= used_ref[0])
    def _():
        ys_ref[...] = jnp.zeros_like(ys_ref)


def _experts(hs, block_e, first, used, w1, b1, w2, b2):
    tmb = MOE_TILE
    n_blocks = hs.shape[0] // tmb
    d_expert = w2.shape[1]
    b1r = b1.reshape(N_EXPERTS, 1, 2 * d_expert)
    b2r = b2.reshape(N_EXPERTS, 1, D_MODEL)
    grid_spec = pltpu.PrefetchScalarGridSpec(
        num_scalar_prefetch=3,
        grid=(n_blocks,),
        in_specs=[pl.BlockSpec((tmb, D_MODEL), lambda i, be, fi, us: (i, 0)),
                  pl.BlockSpec((1, D_MODEL, 2 * d_expert), lambda i, be, fi, us: (be[i], 0, 0)),
                  pl.BlockSpec((1, 1, 2 * d_expert), lambda i, be, fi, us: (be[i], 0, 0)),
                  pl.BlockSpec((1, d_expert, D_MODEL), lambda i, be, fi, us: (be[i], 0, 0)),
                  pl.BlockSpec((1, 1, D_MODEL), lambda i, be, fi, us: (be[i], 0, 0))],
        out_specs=pl.BlockSpec((tmb, D_MODEL), lambda i, be, fi, us: (i, 0)),
        scratch_shapes=[pltpu.VMEM((D_MODEL, 2 * d_expert), BF16), pltpu.VMEM((d_expert, D_MODEL), BF16)],
    )
    return pl.pallas_call(
        _expert_kernel,
        grid_spec=grid_spec,
        out_shape=jax.ShapeDtypeStruct((n_blocks * tmb, D_MODEL), F32),
        compiler_params=pltpu.CompilerParams(dimension_semantics=("arbitrary",),
                                             vmem_limit_bytes=VMEM_LIMIT),
        name="experts",
    )(block_e, first, used, hs, w1, b1r, w2, b2r)


def _route_tables(top_idx, n_rows):
    tmb = MOE_TILE
    nk = n_rows * TOP_K
    n_blocks = nk // tmb + N_EXPERTS
    experts = jnp.arange(N_EXPERTS, dtype=jnp.int32)
    onehot = top_idx[:, :, None] == experts
    member = jnp.any(onehot, axis=1).astype(jnp.int32)
    counts = jnp.sum(member, axis=0)
    rank = jnp.cumsum(member, axis=0) - member
    padded = (counts + tmb - 1) // tmb * tmb
    pad_end = jnp.cumsum(padded)
    pad_start = pad_end - padded
    start = jnp.cumsum(counts) - counts
    slot_pos = jnp.sum(jnp.where(onehot, (pad_start + rank)[:, None, :], 0), axis=-1).astype(jnp.int32)
    order = jnp.argsort(top_idx.reshape(-1), stable=True).astype(jnp.int32)
    block_start = jnp.arange(n_blocks, dtype=jnp.int32) * tmb
    block_e = jnp.minimum(jnp.sum((pad_end[None, :] <= block_start[:, None]).astype(jnp.int32), axis=1),
                          N_EXPERTS - 1)
    row_e = jnp.repeat(block_e, tmb)
    in_group = jnp.arange(n_blocks * tmb, dtype=jnp.int32) - pad_start[row_e]
    valid = in_group < counts[row_e]
    sorted_slot = jnp.clip(start[row_e] + in_group, 0, nk - 1)
    row_tok = jnp.where(valid, order[sorted_slot] // TOP_K, 0).astype(jnp.int32)
    first = jnp.concatenate([jnp.ones((1,), jnp.int32),
                             (block_e[1:] != block_e[:-1]).astype(jnp.int32)])
    used = (pad_end[-1] // tmb).astype(jnp.int32).reshape(1)
    return row_tok, slot_pos, block_e.astype(jnp.int32), first, used


SC_CORES = 2
SC_SUBCORES = 16
SC_VMEM_ROWS = 112


def _gather_rows(table, idx):
    n, d = idx.shape[0], table.shape[1]
    workers = SC_CORES * SC_SUBCORES
    per_worker = n // workers
    assert n % workers == 0
    chunk = max(c for c in range(8, SC_VMEM_ROWS // 2 + 1, 8) if per_worker % (2 * c) == 0)
    mesh = plsc.VectorSubcoreMesh(core_axis_name="core", subcore_axis_name="subcore",
                                  num_cores=SC_CORES, num_subcores=SC_SUBCORES)
    buffer_types = [pltpu.VMEM((chunk,), jnp.int32), pltpu.VMEM((chunk, d), table.dtype),
                    pltpu.SemaphoreType.DMA, pltpu.SemaphoreType.DMA]

    @functools.partial(pl.kernel, mesh=mesh, out_type=jax.ShapeDtypeStruct((n, d), table.dtype),
                       scratch_types=buffer_types + buffer_types)
    def gather(table_hbm, idx_hbm, out_hbm, idx_a, rows_a, gsem_a, wsem_a, idx_b, rows_b, gsem_b, wsem_b):
        worker = lax.axis_index("subcore") * SC_CORES + lax.axis_index("core")

        def start_gather(c, idx_vmem, rows_vmem, sem):
            base = pl.multiple_of(worker * per_worker + c * chunk, 8)
            pltpu.sync_copy(idx_hbm.at[pl.ds(base, chunk)], idx_vmem)
            return pltpu.async_copy(table_hbm.at[idx_vmem], rows_vmem, sem)

        def start_write(c, rows_vmem, sem):
            base = pl.multiple_of(worker * per_worker + c * chunk, 8)
            return pltpu.async_copy(rows_vmem, out_hbm.at[pl.ds(base, chunk)], sem)

        @pl.loop(0, per_worker // (2 * chunk))
        def _(pair):
            gather_a = start_gather(2 * pair, idx_a, rows_a, gsem_a)
            gather_b = start_gather(2 * pair + 1, idx_b, rows_b, gsem_b)
            gather_a.wait()
            write_a = start_write(2 * pair, rows_a, wsem_a)
            gather_b.wait()
            write_b = start_write(2 * pair + 1, rows_b, wsem_b)
            write_a.wait()
            write_b.wait()

    return gather(table, idx)


def _combine_kernel(x1_ref, ys_ref, gate_ref, nf_ref, yp_ref, ys_out_ref, *, prompt_tiles):
    acc = x1_ref[...]
    gate = gate_ref[...]
    for j in range(TOP_K):
        acc = acc + gate[:, j:j + 1] * ys_ref[j]
    y = _rms(acc, nf_ref[...])
    i = pl.program_id(0)

    @pl.when(i < prompt_tiles)
    def _():
        yp_ref[0] = y

    @pl.when(i >= prompt_tiles)
    def _():
        ys_out_ref[...] = y


def _combine(x1, ys_slots, gate, norm_final, batch, t_pad, seq):
    rows = x1.shape[0]
    tm = ROW_TILE
    assert t_pad - seq == tm, "one leading tile of padding + meta rows per sequence"
    per_seq = t_pad // tm
    prompt_tiles = batch * per_seq
    n_sample = rows - batch * t_pad
    nf = norm_final.reshape(1, D_MODEL)

    def prompt_map(i):
        j = jnp.minimum(i, prompt_tiles - 1)
        return (j // per_seq, jnp.maximum(j % per_seq - 1, 0), 0)

    return pl.pallas_call(
        functools.partial(_combine_kernel, prompt_tiles=prompt_tiles),
        grid=(rows // tm,),
        in_specs=[pl.BlockSpec((tm, D_MODEL), lambda i: (i, 0)),
                  pl.BlockSpec((TOP_K, tm, D_MODEL), lambda i: (0, i, 0)),
                  pl.BlockSpec((tm, TOP_K), lambda i: (i, 0)),
                  pl.BlockSpec((1, D_MODEL), lambda i: (0, 0))],
        out_specs=[pl.BlockSpec((1, tm, D_MODEL), prompt_map),
                   pl.BlockSpec((tm, D_MODEL), lambda i: (jnp.maximum(i - prompt_tiles, 0), 0))],
        out_shape=[jax.ShapeDtypeStruct((batch, seq, D_MODEL), F32),
                   jax.ShapeDtypeStruct((n_sample, D_MODEL), F32)],
        compiler_params=pltpu.CompilerParams(dimension_semantics=("arbitrary",),
                                             vmem_limit_bytes=VMEM_LIMIT),
        name="combine_norm",
    )(x1, ys_slots, gate, nf)


def kernel(x_prompt, x_sample, cache_k, cache_v, state_wkv, state_shift, page_table, meta, norm_mix, w_in, sb_bias, rw_mu, rw_w0, rw_decay_up, rw_a0, rw_aaa_up, rw_gate_up, rw_k_k, rw_k_a, rw_r_k, rw_ln_w, rw_ln_b, w_gate, b_gate, w_branch, w_out, norm_ffn, w_router, b_router, w1, b1, w2, b2, norm_final):
    assert w_in.shape[0] == 1, "single layer"
    batch, seq, _ = x_prompt.shape
    n_seq, n_new, _ = x_sample.shape
    t_real = N_META + seq
    t_pad = -(-t_real // ATT_TILE) * ATT_TILE
    pad = t_pad - t_real
    first_tok = pad + N_META
    assert t_pad % RWKV_CHUNK == 0 and (batch * t_pad) % ROW_TILE == 0 and (n_seq * n_new) % ROW_TILE == 0
    row_s = batch * t_pad

    x_all, qb, k_all, v_all, kb, vb, p_all, gates = _projections(
        x_prompt, x_sample.reshape(n_seq * n_new, D_MODEL), meta, t_pad, norm_mix[0], w_in[0], w_gate[0],
        b_gate[0])

    bias = sb_bias[0].astype(F32) * LOG2E
    attn_p = _prompt_attention(qb, kb, vb, bias, batch, t_pad, pad)
    attn_s = _paged_attention(qb, k_all, v_all, cache_k, cache_v, page_table, bias, row_s, n_seq, n_new)

    rw = (rw_mu[0], rw_w0[0], rw_decay_up[0], rw_a0[0], rw_aaa_up[0], rw_gate_up[0],
          rw_k_k[0], rw_k_a[0], rw_r_k[0], rw_ln_w[0], rw_ln_b[0])
    y_p, wkv_p = _rwkv(p_all, 0, batch, t_pad, RWKV_CHUNK,
                       jnp.zeros((batch, RW_PROJ), F32),
                       jnp.zeros((batch, HEADS, HEAD_DIM, HEAD_DIM), F32), rw)
    y_s, wkv_s = _rwkv(p_all, row_s, n_seq, n_new, n_new, state_shift[0], state_wkv[0], rw)

    x1, h2, top_idx, top_gate = _merge_and_route(x_all, attn_p, attn_s, y_p, y_s, gates, w_branch[0],
                                                 w_out[0], norm_ffn[0], w_router[0], b_router[0])

    n_rows = x_all.shape[0]
    row_tok, slot_pos, block_e, first, used = _route_tables(top_idx, n_rows)
    hs = _gather_rows(h2, row_tok)
    ys = _experts(hs, block_e, first, used, w1[0], b1[0], w2[0], b2[0])
    ys_slots = _gather_rows(ys, slot_pos.T.reshape(-1)).reshape(TOP_K, n_rows, D_MODEL)
    y_prompt, y_sample = _combine(x1, ys_slots, top_gate, norm_final, batch, t_pad, seq)
    y_sample = y_sample.reshape(n_seq, n_new, D_MODEL)

    def prompt_rows(a, lo):
        return jnp.stack([a[b * t_pad + lo:(b + 1) * t_pad] for b in range(batch)])

    k_p = prompt_rows(k_all, pad).reshape(1, batch, t_real, HEADS, HEAD_DIM)
    v_p = prompt_rows(v_all, pad).reshape(1, batch, t_real, HEADS, HEAD_DIM)
    shift_p = prompt_rows(p_all, t_pad - 1).reshape(1, batch, RW_PROJ)
    k_s = k_all[row_s:].reshape(1, n_seq, n_new, HEADS, HEAD_DIM)
    v_s = v_all[row_s:].reshape(1, n_seq, n_new, HEADS, HEAD_DIM)
    shift_s = p_all[row_s:].reshape(n_seq, n_new, RW_PROJ)[:, -1].reshape(1, n_seq, RW_PROJ)
    return (y_prompt, y_sample, k_p, v_p, wkv_p[None], shift_p, k_s, v_s, wkv_s[None], shift_s)
```
